```python
import functools
import jax, jax.numpy as jnp
from jax import lax
import numpy as np

D_MODEL = 1024
BATCH = 8
SEQ = 4096
DEPTH = 1
DEC_BATCH = 128
DEC_SEQ = 1
PAST_LEN = 8192
PAGE_SIZE = 128

SSM_EXPAND = 2
D_INNER = SSM_EXPAND * D_MODEL
SSM_HEAD_DIM = 64
SSM_HEADS = D_INNER // SSM_HEAD_DIM
SSM_GROUPS = 4
SSM_HPG = SSM_HEADS // SSM_GROUPS
D_STATE = 128
CONV_W = 4
CONV_DIM = D_INNER + 2 * SSM_GROUPS * D_STATE
SSD_CHUNK = 128
SB_HEAD_DIM = 64
SB_HEADS = D_MODEL // SB_HEAD_DIM
SB_KV_HEADS = 4
SB_REP = SB_HEADS // SB_KV_HEADS
SB_Q_BLOCK = 128
SB_SCALE = SB_HEAD_DIM ** -0.5
SB_BIAS_INIT = -7.0
D_FF = 4 * D_MODEL
DEEPNORM_ALPHA = (2.0 * DEPTH) ** 0.25
DEEPNORM_BETA = (8.0 * DEPTH) ** -0.25
LN_EPS = 1e-5
RMS_EPS = 1e-5
IN_SIZES = (D_INNER, CONV_DIM, SSM_HEADS, SB_HEADS * SB_HEAD_DIM,
            SB_KV_HEADS * SB_HEAD_DIM, SB_KV_HEADS * SB_HEAD_DIM, 2 * D_MODEL)
IN_DIM = sum(IN_SIZES)
IN_SPLITS = tuple(int(s) for s in np.cumsum(IN_SIZES)[:-1])

kernel_name = "hybrid_ssd_stickbreak_deepnorm_step"


def layer_norm(x, g, b):
    xf = x.astype(jnp.float32)
    mu = jnp.mean(xf, -1, keepdims=True)
    var = jnp.mean(jnp.square(xf - mu), -1, keepdims=True)
    return ((xf - mu) * lax.rsqrt(var + LN_EPS) * g + b).astype(x.dtype)


def gated_rms_norm(y, z, w):
    h = y.astype(jnp.float32) * jax.nn.silu(z.astype(jnp.float32))
    h = h * lax.rsqrt(jnp.mean(h * h, -1, keepdims=True) + RMS_EPS)
    return (h * w).astype(z.dtype)


def causal_dwconv(xbc, prev, w, b):
    seqlen = xbc.shape[1]
    xp = jnp.concatenate([prev.astype(xbc.dtype), xbc], axis=1)
    out = b + xp[:, 0:seqlen] * w[0]
    for tap in range(1, CONV_W):
        out = out + xp[:, tap:tap + seqlen] * w[tap]
    return jax.nn.silu(out), xp[:, -(CONV_W - 1):]


def ssd_scan(x, dt, a, bm, cm, h0):
    bsz, seqlen = x.shape[:2]
    q = min(SSD_CHUNK, seqlen)
    nc = seqlen // q
    xc = x.reshape(bsz, nc, q, SSM_GROUPS, SSM_HPG, SSM_HEAD_DIM)
    dtc = dt.reshape(bsz, nc, q, SSM_GROUPS, SSM_HPG)
    bc = bm.reshape(bsz, nc, q, SSM_GROUPS, D_STATE)
    cc = cm.reshape(bsz, nc, q, SSM_GROUPS, D_STATE)
    cum = jnp.cumsum(dtc * a.reshape(SSM_GROUPS, SSM_HPG), axis=2)
    cum_t = jnp.moveaxis(cum, 2, -1)
    causal = jnp.tril(jnp.ones((q, q), dtype=bool))
    seg = cum_t[..., :, None] - cum_t[..., None, :]
    decay = jnp.exp(jnp.where(causal, seg, -jnp.inf))
    xdt = xc * dtc[..., None]
    cb = jnp.einsum('bclgn,bcsgn->bcgls', cc, bc)
    y_diag = jnp.einsum('bcgrls,bcsgrp->bclgrp', decay * cb[:, :, :, None], xdt)
    to_end = jnp.exp(cum[:, :, -1:] - cum)
    st = jnp.einsum('bclgn,bclgrp->bcgrpn', bc, xdt * to_end[..., None])
    chunk_decay = jnp.exp(cum[:, :, -1])

    def carry(h, inp):
        s_c, d_c = inp
        return h * d_c[..., None, None] + s_c, h

    h_last, h_in = lax.scan(
        carry, h0.reshape(bsz, SSM_GROUPS, SSM_HPG, SSM_HEAD_DIM, D_STATE),
        (jnp.moveaxis(st, 1, 0), jnp.moveaxis(chunk_decay, 1, 0)))
    h_in = jnp.moveaxis(h_in, 0, 1)
    y_off = jnp.einsum('bclgn,bcgrpn->bclgrp', cc, h_in) * jnp.exp(cum)[..., None]
    y = (y_diag + y_off).reshape(bsz, seqlen, SSM_HEADS, SSM_HEAD_DIM)
    return y, h_last.reshape(bsz, SSM_HEADS, SSM_HEAD_DIM, D_STATE)


def stick_breaking(q, k, v, bias, q_pos, k_pos):
    z = jnp.einsum('bqgrd,bkgd->bgrqk', q, k).astype(jnp.float32) * SB_SCALE
    z = z + bias.astype(jnp.float32).reshape(SB_KV_HEADS, SB_REP)[:, :, None, None]
    mask = k_pos[None, :] < q_pos[:, None]
    log_beta = jax.nn.log_sigmoid(z)
    log_keep = jnp.where(mask, log_beta - z, 0.0)
    after = lax.cumsum(log_keep, axis=4, reverse=True) - log_keep
    w = jnp.where(mask, jnp.exp(log_beta + after), 0.0)
    return jnp.einsum('bgrqk,bkgd->bqgrd', w.astype(v.dtype), v)


def sb_prompt(q, k, v, bias):
    bsz, seqlen = q.shape[:2]
    nb = seqlen // SB_Q_BLOCK
    qb = jnp.moveaxis(q.reshape(bsz, nb, SB_Q_BLOCK, SB_KV_HEADS, SB_REP, SB_HEAD_DIM), 1, 0)
    k_pos = jnp.arange(seqlen)

    def block(args):
        q_blk, i = args
        q_pos = i * SB_Q_BLOCK + jnp.arange(SB_Q_BLOCK)
        return stick_breaking(q_blk, k, v, bias, q_pos, k_pos)

    o = lax.map(block, (qb, jnp.arange(nb)))
    return jnp.moveaxis(o, 0, 1).reshape(bsz, seqlen, SB_HEADS * SB_HEAD_DIM)


def sb_decode(q, k, v, bias, k_past, v_past):
    past = k_past.shape[1]
    tq = q.shape[1]
    k_all = jnp.concatenate([k_past.astype(k.dtype), k], axis=1)
    v_all = jnp.concatenate([v_past.astype(v.dtype), v], axis=1)
    o = stick_breaking(q, k_all, v_all, bias, past + jnp.arange(tq), jnp.arange(past + tq))
    return o.reshape(q.shape[0], tq, SB_HEADS * SB_HEAD_DIM)


def gather_pages(cache, page_table):
    g = cache[page_table]
    return g.reshape(page_table.shape[0], page_table.shape[1] * PAGE_SIZE, SB_KV_HEADS, SB_HEAD_DIM)


def trunk_layer(x, conv_prev, ssm_h0, attend, w_in, conv_w, conv_b, dt_bias, a_log, d_skip,
                ssm_norm_w, w_ssm_br, w_attn_br, sb_logit_bias, w_out, ln1_g, ln1_b,
                w_up, w_down, ln2_g, ln2_b):
    bsz, seqlen, _ = x.shape
    f32 = jnp.float32
    proj = x @ w_in
    z, xbc, dt, q, k, v, gates = jnp.split(proj, IN_SPLITS, axis=-1)
    xbc, conv_state = causal_dwconv(xbc, conv_prev, conv_w, conv_b)
    xs, bm, cm = jnp.split(xbc, [D_INNER, D_INNER + SSM_GROUPS * D_STATE], axis=-1)
    dt = jax.nn.softplus(dt.astype(f32) + dt_bias.astype(f32))
    a = -jnp.exp(a_log.astype(f32))
    xs_h = xs.reshape(bsz, seqlen, SSM_HEADS, SSM_HEAD_DIM).astype(f32)
    y, h_last = ssd_scan(xs_h, dt, a,
                         bm.reshape(bsz, seqlen, SSM_GROUPS, D_STATE).astype(f32),
                         cm.reshape(bsz, seqlen, SSM_GROUPS, D_STATE).astype(f32),
                         ssm_h0.astype(f32))
    y = y + xs_h * d_skip.astype(f32)[:, None]
    y = gated_rms_norm(y.reshape(bsz, seqlen, D_INNER), z, ssm_norm_w)
    ssm_out = y @ w_ssm_br
    q = q.reshape(bsz, seqlen, SB_KV_HEADS, SB_REP, SB_HEAD_DIM)
    k = k.reshape(bsz, seqlen, SB_KV_HEADS, SB_HEAD_DIM)
    v = v.reshape(bsz, seqlen, SB_KV_HEADS, SB_HEAD_DIM)
    attn_out = attend(q, k, v, sb_logit_bias) @ w_attn_br
    g_ssm, g_attn = jnp.split(gates, 2, axis=-1)
    mixed = jax.nn.sigmoid(g_ssm) * ssm_out + jax.nn.sigmoid(g_attn) * attn_out
    x = layer_norm(DEEPNORM_ALPHA * x + mixed @ w_out, ln1_g, ln1_b)
    hid = jnp.square(jax.nn.relu(x @ w_up))
    x = layer_norm(DEEPNORM_ALPHA * x + hid @ w_down, ln2_g, ln2_b)
    return x, k, v, h_last, conv_state


def setup_inputs(seed: int = 0) -> dict:
    key = jax.random.key(seed)
    ks = jax.random.split(key, 24)
    f32 = jnp.float32
    n_pages = PAST_LEN // PAGE_SIZE
    n_used = DEC_BATCH * n_pages
    n_pool = n_used + max(1, n_used // 4)

    def nrm(k, shape, scale):
        return jax.random.normal(k, shape, f32) * scale

    x_prompt = nrm(ks[0], (BATCH, SEQ, D_MODEL), 1.0)
    x_sample = nrm(ks[1], (DEC_BATCH, DEC_SEQ, D_MODEL), 1.0)
    cache_k = nrm(ks[2], (DEPTH, n_pool, PAGE_SIZE, SB_KV_HEADS, SB_HEAD_DIM), 1.0)
    cache_v = nrm(ks[3], (DEPTH, n_pool, PAGE_SIZE, SB_KV_HEADS, SB_HEAD_DIM), 1.0)
    state_ssm = nrm(ks[4], (DEPTH, DEC_BATCH, SSM_HEADS, SSM_HEAD_DIM, D_STATE), 0.5)
    state_conv = nrm(ks[5], (DEPTH, DEC_BATCH, CONV_W - 1, CONV_DIM), 1.0)
    page_table = jax.random.permutation(ks[6], n_pool)[:n_used].reshape(
        DEC_BATCH, n_pages).astype(jnp.int32)

    w_in = nrm(ks[7], (DEPTH, D_MODEL, IN_DIM), D_MODEL ** -0.5)
    conv_w = nrm(ks[8], (DEPTH, CONV_W, CONV_DIM), CONV_W ** -0.5)
    conv_b = nrm(ks[9], (DEPTH, CONV_DIM), 0.01)
    u = jax.random.uniform(ks[10], (DEPTH, SSM_HEADS), f32)
    dt0 = jnp.exp(u * (jnp.log(0.1) - jnp.log(0.001)) + jnp.log(0.001))
    dt_bias = dt0 + jnp.log(-jnp.expm1(-dt0))
    a_log = jnp.log(jax.random.uniform(ks[11], (DEPTH, SSM_HEADS), f32, 1.0, 16.0))
    d_skip = 1.0 + nrm(ks[12], (DEPTH, SSM_HEADS), 0.1)
    ssm_norm_w = 1.0 + nrm(ks[13], (DEPTH, D_INNER), 0.02)
    w_ssm_br = nrm(ks[14], (DEPTH, D_INNER, D_MODEL), D_INNER ** -0.5)
    w_attn_br = nrm(ks[15], (DEPTH, SB_HEADS * SB_HEAD_DIM, D_MODEL), (SB_HEADS * SB_HEAD_DIM) ** -0.5)
    sb_logit_bias = SB_BIAS_INIT + nrm(ks[23], (DEPTH, SB_HEADS), 0.5)
    w_out = nrm(ks[16], (DEPTH, D_MODEL, D_MODEL), DEEPNORM_BETA * D_MODEL ** -0.5)
    ln1_g = 1.0 + nrm(ks[17], (DEPTH, D_MODEL), 0.02)
    ln1_b = nrm(ks[18], (DEPTH, D_MODEL), 0.02)
    w_up = nrm(ks[19], (DEPTH, D_MODEL, D_FF), D_MODEL ** -0.5)
    w_down = nrm(ks[20], (DEPTH, D_FF, D_MODEL), DEEPNORM_BETA * D_FF ** -0.5)
    ln2_g = 1.0 + nrm(ks[21], (DEPTH, D_MODEL), 0.02)
    ln2_b = nrm(ks[22], (DEPTH, D_MODEL), 0.02)
    return {"x_prompt": x_prompt, "x_sample": x_sample, "cache_k": cache_k, "cache_v": cache_v,
            "state_ssm": state_ssm, "state_conv": state_conv, "page_table": page_table,
            "w_in": w_in, "conv_w": conv_w, "conv_b": conv_b, "dt_bias": dt_bias, "a_log": a_log,
            "d_skip": d_skip, "ssm_norm_w": ssm_norm_w, "w_ssm_br": w_ssm_br, "w_attn_br": w_attn_br,
            "sb_logit_bias": sb_logit_bias, "w_out": w_out, "ln1_g": ln1_g, "ln1_b": ln1_b,
            "w_up": w_up, "w_down": w_down, "ln2_g": ln2_g, "ln2_b": ln2_b}


def reference(x_prompt, x_sample, cache_k, cache_v, state_ssm, state_conv, page_table,
              w_in, conv_w, conv_b, dt_bias, a_log, d_skip, ssm_norm_w, w_ssm_br, w_attn_br,
              sb_logit_bias, w_out, ln1_g, ln1_b, w_up, w_down, ln2_g, ln2_b):
    bp = x_prompt.shape[0]
    yp, ys = x_prompt, x_sample
    kp_l, vp_l, hp_l, cp_l = [], [], [], []
    ks_l, vs_l, hs_l, cs_l = [], [], [], []
    for layer in range(DEPTH):
        params = (w_in[layer], conv_w[layer], conv_b[layer], dt_bias[layer], a_log[layer],
                  d_skip[layer], ssm_norm_w[layer], w_ssm_br[layer], w_attn_br[layer],
                  sb_logit_bias[layer], w_out[layer], ln1_g[layer], ln1_b[layer],
                  w_up[layer], w_down[layer], ln2_g[layer], ln2_b[layer])
        conv0 = jnp.zeros((bp, CONV_W - 1, CONV_DIM), yp.dtype)
        h0 = jnp.zeros((bp, SSM_HEADS, SSM_HEAD_DIM, D_STATE), jnp.float32)
        yp, kp, vp, hp, cp = trunk_layer(yp, conv0, h0, sb_prompt, *params)
        attend = functools.partial(sb_decode,
                                   k_past=gather_pages(cache_k[layer], page_table),
                                   v_past=gather_pages(cache_v[layer], page_table))
        ys, ksm, vsm, hsm, csm = trunk_layer(ys, state_conv[layer], state_ssm[layer], attend, *params)
        kp_l.append(kp)
        vp_l.append(vp)
        hp_l.append(hp.astype(x_prompt.dtype))
        cp_l.append(cp)
        ks_l.append(ksm)
        vs_l.append(vsm)
        hs_l.append(hsm.astype(state_ssm.dtype))
        cs_l.append(csm.astype(state_conv.dtype))
    return (yp, ys, jnp.stack(kp_l), jnp.stack(vp_l), jnp.stack(hp_l), jnp.stack(cp_l),
            jnp.stack(ks_l), jnp.stack(vs_l), jnp.stack(hs_l), jnp.stack(cs_l))
```

```python
import functools

import jax
import jax.numpy as jnp
from jax import lax
from jax.experimental import pallas as pl
from jax.experimental.pallas import tpu as pltpu

F32 = jnp.float32
BF16 = jnp.bfloat16

D_MODEL = 1024
D_INNER = 2048
SSM_HEAD_DIM = 64
SSM_HEADS = 32
SSM_GROUPS = 4
SSM_HPG = SSM_HEADS // SSM_GROUPS
D_STATE = 128
CONV_W = 4
CONV_DIM = D_INNER + 2 * SSM_GROUPS * D_STATE
SSD_CHUNK = 128
SB_HEAD_DIM = 64
SB_HEADS = 16
SB_KV_HEADS = 4
SB_REP = SB_HEADS // SB_KV_HEADS
SB_SCALE = SB_HEAD_DIM ** -0.5
D_FF = 4 * D_MODEL
PAGE_SIZE = 128
DEEPNORM_ALPHA = 2.0 ** 0.25
LN_EPS = 1e-5
RMS_EPS = 1e-5
IN_SIZES = (D_INNER, CONV_DIM, SSM_HEADS, SB_HEADS * SB_HEAD_DIM,
            SB_KV_HEADS * SB_HEAD_DIM, SB_KV_HEADS * SB_HEAD_DIM, 2 * D_MODEL)

LANES = 128
VMEM_LIMIT_BYTES = 56 * 1024 * 1024

PROJ_TN = 512
_TZ = D_INNER // PROJ_TN
_TX = CONV_DIM // PROJ_TN
_TQ = (SB_HEADS * SB_HEAD_DIM) // PROJ_TN
_TG = (2 * D_MODEL) // PROJ_TN
_J_XBC = _TZ
_J_Q = _J_XBC + _TX
_J_G = _J_Q + _TQ
_J_DT = _J_G + _TG
_J_KV = _J_DT + 1
PROJ_TILES = _J_KV + 1
KV_W = SB_KV_HEADS * SB_HEAD_DIM

ATT_QB = 128
ATT_KB = 256
DEC_PPS = 16

_NT = (((1,), (1,)), ((), ()))


def _dot(a, b):
    return jnp.dot(a, b, preferred_element_type=F32)


def _dot_nt(a, b):
    return lax.dot_general(a, b, _NT, preferred_element_type=F32)


def _sigmoid(x):
    return 1.0 / (1.0 + jnp.exp(-x))


def _silu(x):
    return x * _sigmoid(x)


def _layer_norm(r, g, b):
    mu = jnp.mean(r, axis=-1, keepdims=True)
    d = r - mu
    var = jnp.mean(d * d, axis=-1, keepdims=True)
    return d * lax.rsqrt(var + LN_EPS) * g + b


def _split3(x):
    hi = x.astype(BF16)
    r = x - hi.astype(F32)
    mid = r.astype(BF16)
    lo = (r - mid.astype(F32)).astype(BF16)
    return hi, mid, lo


def _params(sem):
    return pltpu.CompilerParams(dimension_semantics=sem, vmem_limit_bytes=VMEM_LIMIT_BYTES)


def _inproj_kernel(x_ref, w_ref, wkv_ref, *refs, key_blocks):
    if key_blocks:
        z_ref, xbc_ref, q_ref, g_ref, dt_ref, kt_ref, vt_ref, kb_ref, vb_ref, xb_ref = refs
    else:
        z_ref, xbc_ref, q_ref, g_ref, dt_ref, kt_ref, vt_ref, xb_ref = refs
    j = pl.program_id(1)

    @pl.when(j == 0)
    def _():
        xb_ref[...] = x_ref[...].astype(BF16)

    def mm():
        return _dot(xb_ref[...], w_ref[...])

    @pl.when(j < _J_XBC)
    def _():
        z_ref[...] = mm()

    @pl.when((j >= _J_XBC) & (j < _J_Q))
    def _():
        xbc_ref[...] = mm()

    @pl.when((j >= _J_Q) & (j < _J_G))
    def _():
        q_ref[...] = (mm() * SB_SCALE).astype(BF16)

    @pl.when((j >= _J_G) & (j < _J_DT))
    def _():
        g_ref[...] = mm()

    @pl.when(j == _J_DT)
    def _():
        dt_ref[...] = mm()[:, :LANES]

    @pl.when(j == _J_KV)
    def _():
        kvt = _dot_nt(wkv_ref[...], xb_ref[...])
        kt_ref[0] = kvt[:KV_W]
        vt_ref[0] = kvt[KV_W:]
        if key_blocks:
            for c in range(key_blocks):
                csl = slice(c * ATT_KB, (c + 1) * ATT_KB)
                for g in range(SB_KV_HEADS):
                    kb_ref[0, c, g] = kvt[g * SB_HEAD_DIM:(g + 1) * SB_HEAD_DIM, csl].astype(BF16)
                    vb_ref[0, c, g] = kvt[KV_W + g * SB_HEAD_DIM:KV_W + (g + 1) * SB_HEAD_DIM, csl].astype(BF16)


def _prep_w_in(w_in):
    splits = []
    off = 0
    for s in IN_SIZES:
        splits.append(w_in[:, off:off + s])
        off += s
    wz, wxbc, wdt, wq, wk, wv, wg = splits
    wdt = jnp.pad(wdt, ((0, 0), (0, PROJ_TN - SSM_HEADS)))
    w_r = jnp.concatenate([wz, wxbc, wq, wg, wdt], axis=1).astype(BF16)
    wkv_t = jnp.concatenate([wk, wv], axis=1).T.astype(BF16)
    return w_r, wkv_t


def _in_projection(x, w_r, wkv_t, bsz, seqlen, tm, with_key_blocks):
    t = x.shape[0]
    tpb = seqlen // tm
    key_blocks = tm // ATT_KB if with_key_blocks else 0

    def tile(lo, n):
        return lambda i, j: (i, jnp.clip(j - lo, 0, n - 1))

    row = lambda i, j: (i, 0)
    seq3 = lambda i, j: (i // tpb, 0, i % tpb)
    seq5 = lambda i, j: (i // tpb, i % tpb, 0, 0, 0)
    out_shape = [
        jax.ShapeDtypeStruct((t, D_INNER), F32),
        jax.ShapeDtypeStruct((t, CONV_DIM), F32),
        jax.ShapeDtypeStruct((t, SB_HEADS * SB_HEAD_DIM), BF16),
        jax.ShapeDtypeStruct((t, 2 * D_MODEL), F32),
        jax.ShapeDtypeStruct((t, LANES), F32),
        jax.ShapeDtypeStruct((bsz, KV_W, seqlen), F32),
        jax.ShapeDtypeStruct((bsz, KV_W, seqlen), F32),
    ]
    out_specs = [
        pl.BlockSpec((tm, PROJ_TN), tile(0, _TZ)),
        pl.BlockSpec((tm, PROJ_TN), tile(_J_XBC, _TX)),
        pl.BlockSpec((tm, PROJ_TN), tile(_J_Q, _TQ)),
        pl.BlockSpec((tm, PROJ_TN), tile(_J_G, _TG)),
        pl.BlockSpec((tm, LANES), row),
        pl.BlockSpec((1, KV_W, tm), seq3),
        pl.BlockSpec((1, KV_W, tm), seq3),
    ]
    if key_blocks:
        blk = (bsz, seqlen // ATT_KB, SB_KV_HEADS, SB_HEAD_DIM, ATT_KB)
        out_shape += [jax.ShapeDtypeStruct(blk, BF16)] * 2
        out_specs += [pl.BlockSpec((1, key_blocks, SB_KV_HEADS, SB_HEAD_DIM, ATT_KB), seq5)] * 2
    return pl.pallas_call(
        functools.partial(_inproj_kernel, key_blocks=key_blocks),
        out_shape=tuple(out_shape),
        grid=(t // tm, PROJ_TILES),
        in_specs=[pl.BlockSpec((tm, D_MODEL), row),
                  pl.BlockSpec((D_MODEL, PROJ_TN), lambda i, j: (0, jnp.minimum(j, _J_KV - 1))),
                  pl.BlockSpec((2 * KV_W, D_MODEL), lambda i, j: (0, 0))],
        out_specs=tuple(out_specs),
        scratch_shapes=[pltpu.VMEM((tm, D_MODEL), BF16)],
        compiler_params=_params(("parallel", "arbitrary")),
        name="in_projection",
    )(x, w_r, wkv_t)


def _softplus(x):
    return jnp.maximum(x, 0.0) + jnp.log1p(jnp.exp(-jnp.abs(x)))


def _gated_rms_norm(y, z, w):
    h = y * _silu(z)
    h = h * lax.rsqrt(jnp.mean(h * h, axis=-1, keepdims=True) + RMS_EPS)
    return h * w


def _ssd_prompt_kernel(xbc_ref, dt_ref, z_ref, cw_ref, cb_ref, dtb_ref, alog_ref, dskip_ref, nw_ref,
                       yn_ref, hout_ref, cout_ref,
                       xwin, xc, hst, ybuf, cum_s, ecum_s, toend_s):
    c = pl.program_id(1)
    nc = pl.num_programs(1)
    q = SSD_CHUNK
    tail = CONV_W - 1

    @pl.when(c == 0)
    def _():
        xwin[0:8, :] = jnp.zeros((8, CONV_DIM), F32)
        hst[...] = jnp.zeros(hst.shape, F32)

    xwin[8:8 + q, :] = xbc_ref[...]
    cw = 512
    for j in range(CONV_DIM // cw):
        sl = slice(j * cw, (j + 1) * cw)
        a = cb_ref[:, sl] + cw_ref[0:1, sl] * xwin[8 - tail:8 - tail + q, sl]
        for tap in range(1, CONV_W):
            a = a + cw_ref[tap:tap + 1, sl] * xwin[8 - tail + tap:8 - tail + tap + q, sl]
        xc[:, sl] = _silu(a)

    @pl.when(c == nc - 1)
    def _():
        cout_ref[0] = xwin[8 + q - tail:8 + q, :]

    xwin[0:8, :] = xwin[q:q + 8, :]

    dtv = _softplus(dt_ref[...] + dtb_ref[...])
    neg_a = -jnp.exp(alog_ref[...])
    da = dtv * neg_a
    rows = lax.broadcasted_iota(jnp.int32, (q, q), 0)
    cols = lax.broadcasted_iota(jnp.int32, (q, q), 1)
    causal = rows >= cols
    tri = causal.astype(BF16)
    hi, mid, lo = _split3(da)
    cum = (_dot(tri, lo) + _dot(tri, mid)) + _dot(tri, hi)
    cum_t = cum.T
    cum_last = cum[q - 1:q, :]
    cum_s[...] = cum
    ecum_s[...] = jnp.exp(cum)
    toend_s[...] = jnp.exp(cum_last - cum)
    cdec = jnp.exp(cum_last)

    bm_off = D_INNER
    cm_off = D_INNER + SSM_GROUPS * D_STATE
    for g in range(SSM_GROUPS):
        bg = xc[:, bm_off + g * D_STATE:bm_off + (g + 1) * D_STATE]
        cg = xc[:, cm_off + g * D_STATE:cm_off + (g + 1) * D_STATE]
        bgb = bg.astype(BF16)
        cgb = cg.astype(BF16)
        cb = _dot_nt(cgb, bgb)
        bgtb = bg.T.astype(BF16)
        for r in range(SSM_HPG):
            h = g * SSM_HPG + r
            pair, half = h // 2, (h % 2) * SSM_HEAD_DIM
            hsl = slice(h * SSM_HEAD_DIM, (h + 1) * SSM_HEAD_DIM)
            seg = cum_s[:, h:h + 1] - cum_t[h:h + 1, :]
            decay = jnp.exp(jnp.where(causal, seg, -jnp.inf))
            m = (decay * cb).astype(BF16)
            xh = xc[:, hsl]
            xdt = xh * dtv[:, h:h + 1]
            y_diag = _dot(m, xdt.astype(BF16))
            h_in = hst[pair, :, half:half + SSM_HEAD_DIM]
            y_off = _dot(cgb, h_in.astype(BF16)) * ecum_s[:, h:h + 1]
            st = _dot(bgtb, (xdt * toend_s[:, h:h + 1]).astype(BF16))
            hst[pair, :, half:half + SSM_HEAD_DIM] = h_in * cdec[:, h:h + 1] + st
            ybuf[:, hsl] = (y_diag + y_off) + xh * dskip_ref[:, hsl]

    yn_ref[...] = _gated_rms_norm(ybuf[...], z_ref[...], nw_ref[...]).astype(BF16)

    @pl.when(c == nc - 1)
    def _():
        for pair in range(SSM_HEADS // 2):
            ht = hst[pair].T
            hout_ref[0, 2 * pair] = ht[:SSM_HEAD_DIM]
            hout_ref[0, 2 * pair + 1] = ht[SSM_HEAD_DIM:]


def _pad_lanes(v):
    return jnp.pad(v, (0, LANES - v.shape[0])).reshape(1, LANES)


def _ssd_prompt(xbc, dt, z, conv_w, conv_b, dt_bias, a_log, d_skip, norm_w, bsz, seqlen):
    t = bsz * seqlen
    q = SSD_CHUNK
    nc = seqlen // q
    tok = lambda b, c: (b * nc + c, 0)
    const = lambda b, c: (0, 0)
    out_shape = (
        jax.ShapeDtypeStruct((t, D_INNER), BF16),
        jax.ShapeDtypeStruct((bsz, SSM_HEADS, SSM_HEAD_DIM, D_STATE), F32),
        jax.ShapeDtypeStruct((bsz, CONV_W - 1, CONV_DIM), F32),
    )
    return pl.pallas_call(
        _ssd_prompt_kernel,
        out_shape=out_shape,
        grid=(bsz, nc),
        in_specs=[pl.BlockSpec((q, CONV_DIM), tok),
                  pl.BlockSpec((q, LANES), tok),
                  pl.BlockSpec((q, D_INNER), tok),
                  pl.BlockSpec((CONV_W, CONV_DIM), const),
                  pl.BlockSpec((1, CONV_DIM), const),
                  pl.BlockSpec((1, LANES), const),
                  pl.BlockSpec((1, LANES), const),
                  pl.BlockSpec((1, D_INNER), const),
                  pl.BlockSpec((1, D_INNER), const)],
        out_specs=(pl.BlockSpec((q, D_INNER), tok),
                   pl.BlockSpec((1, SSM_HEADS, SSM_HEAD_DIM, D_STATE), lambda b, c: (b, 0, 0, 0)),
                   pl.BlockSpec((1, CONV_W - 1, CONV_DIM), lambda b, c: (b, 0, 0))),
        scratch_shapes=[pltpu.VMEM((q + 8, CONV_DIM), F32),
                        pltpu.VMEM((q, CONV_DIM), F32),
                        pltpu.VMEM((SSM_HEADS // 2, D_STATE, 2 * SSM_HEAD_DIM), F32),
                        pltpu.VMEM((q, D_INNER), F32),
                        pltpu.VMEM((q, LANES), F32),
                        pltpu.VMEM((q, LANES), F32),
                        pltpu.VMEM((q, LANES), F32)],
        compiler_params=_params(("parallel", "arbitrary")),
        name="ssd_prompt",
    )(xbc, dt, z, conv_w, conv_b.reshape(1, CONV_DIM), _pad_lanes(dt_bias), _pad_lanes(a_log),
      jnp.repeat(d_skip, SSM_HEAD_DIM).reshape(1, D_INNER), norm_w.reshape(1, D_INNER))


def _sb_block(z, sp_mask, tmat, carry, vt_b):
    sp = jnp.maximum(z, 0.0) + jnp.log(1.0 + jnp.exp(-jnp.abs(z)))
    if sp_mask is not None:
        sp = jnp.where(sp_mask, sp, 0.0)
    cum = _dot(sp.astype(BF16), tmat)
    w = jnp.exp(z - sp - cum - carry)
    if sp_mask is not None:
        w = jnp.where(sp_mask, w, 0.0)
    pv = _dot_nt(w.astype(BF16), vt_b)
    return pv, carry + (cum[:, 0:1] + sp[:, 0:1])


def _later_key_matrix(n):
    r = lax.broadcasted_iota(jnp.int32, (n, n), 0)
    c = lax.broadcasted_iota(jnp.int32, (n, n), 1)
    return (r > c).astype(BF16)


def _attn_prompt_kernel(q_ref, k_ref, v_ref, bias_ref, o_ref, acc_ref, carry_ref):
    i = pl.program_id(2)
    rows = SB_REP * ATT_QB
    qb = q_ref[...]
    qs = jnp.concatenate([qb[:, r * SB_HEAD_DIM:(r + 1) * SB_HEAD_DIM] for r in range(SB_REP)], axis=0)
    bias = jnp.concatenate([bias_ref[0]] * (ATT_KB // LANES), axis=1)
    tmat = _later_key_matrix(ATT_KB)

    def block(jb, masked):
        kt_b = k_ref[0, jb, 0]
        vt_b = v_ref[0, jb, 0]
        z = _dot(qs, kt_b) + bias
        mask = None
        if masked:
            t_pos = i * ATT_QB + (lax.broadcasted_iota(jnp.int32, (rows, ATT_KB), 0) % ATT_QB)
            s_pos = jb * ATT_KB + lax.broadcasted_iota(jnp.int32, (rows, ATT_KB), 1)
            mask = s_pos < t_pos
        pv, carry = _sb_block(z, mask, tmat, carry_ref[...], vt_b)
        acc_ref[...] += pv
        carry_ref[...] = carry

    acc_ref[...] = jnp.zeros(acc_ref.shape, F32)
    carry_ref[...] = jnp.zeros(carry_ref.shape, F32)
    jd = (i * ATT_QB) // ATT_KB
    block(jd, True)

    def body(t, _):
        block(jd - 1 - t, False)
        return 0

    lax.fori_loop(0, jd, body, 0)

    for r in range(SB_REP):
        o_ref[:, r * SB_HEAD_DIM:(r + 1) * SB_HEAD_DIM] = acc_ref[r * ATT_QB:(r + 1) * ATT_QB, :].astype(BF16)


def _attn_prompt(qs, kb, vb, bias, bsz, seqlen):
    t = bsz * seqlen
    nq = seqlen // ATT_QB
    nkb = seqlen // ATT_KB
    rows = SB_REP * ATT_QB
    gw = SB_REP * SB_HEAD_DIM
    bias_rows = jnp.broadcast_to(bias.astype(F32).reshape(SB_KV_HEADS, SB_REP, 1, 1),
                                 (SB_KV_HEADS, SB_REP, ATT_QB, LANES)).reshape(SB_KV_HEADS, rows, LANES)
    return pl.pallas_call(
        _attn_prompt_kernel,
        out_shape=jax.ShapeDtypeStruct((t, SB_HEADS * SB_HEAD_DIM), BF16),
        grid=(bsz, SB_KV_HEADS, nq),
        in_specs=[pl.BlockSpec((ATT_QB, gw), lambda b, g, i: (b * nq + i, g)),
                  pl.BlockSpec((1, nkb, 1, SB_HEAD_DIM, ATT_KB), lambda b, g, i: (b, 0, g, 0, 0)),
                  pl.BlockSpec((1, nkb, 1, SB_HEAD_DIM, ATT_KB), lambda b, g, i: (b, 0, g, 0, 0)),
                  pl.BlockSpec((1, rows, LANES), lambda b, g, i: (g, 0, 0))],
        out_specs=pl.BlockSpec((ATT_QB, gw), lambda b, g, i: (b * nq + i, g)),
        scratch_shapes=[pltpu.VMEM((rows, SB_HEAD_DIM), F32),
                        pltpu.VMEM((rows, 1), F32)],
        compiler_params=_params(("parallel", "parallel", "arbitrary")),
        name="attn_prompt",
    )(qs, kb, vb, bias_rows)


def _attn_decode_kernel(pt_ref, q_ref, bias_ref, *refs):
    k_refs = refs[:DEC_PPS]
    v_refs = refs[DEC_PPS:2 * DEC_PPS]
    o_ref, acc_ref, carry_ref = refs[2 * DEC_PPS:]
    s = pl.program_id(1)

    @pl.when(s == 0)
    def _():
        acc_ref[...] = jnp.zeros(acc_ref.shape, F32)
        carry_ref[...] = jnp.zeros(carry_ref.shape, F32)

    hq = lax.broadcasted_iota(jnp.int32, (SB_HEADS, KV_W), 0) // SB_REP
    gl = lax.broadcasted_iota(jnp.int32, (SB_HEADS, KV_W), 1) // SB_HEAD_DIM
    own = hq == gl
    q_t = jnp.concatenate([q_ref[0]] * SB_KV_HEADS, axis=1)
    q_bd = jnp.where(own, q_t, jnp.zeros_like(q_t))
    bias = bias_ref[...]
    tmat = _later_key_matrix(PAGE_SIZE)
    for m in range(DEC_PPS):
        kt_b = k_refs[m][0].astype(BF16)
        vt_b = v_refs[m][0].astype(BF16)
        z = _dot(q_bd, kt_b) + bias
        pv, carry = _sb_block(z, None, tmat, carry_ref[...], vt_b)
        acc_ref[...] += pv
        carry_ref[...] = carry

    @pl.when(s == pl.num_programs(1) - 1)
    def _():
        acc = jnp.where(own, acc_ref[...], 0.0)
        out = acc[:, :SB_HEAD_DIM]
        for g in range(1, SB_KV_HEADS):
            out = out + acc[:, g * SB_HEAD_DIM:(g + 1) * SB_HEAD_DIM]
        o_ref[0] = out


def _attn_decode(qs, cache_k, cache_v, page_table, bias):
    db, n_pages = page_table.shape
    n_pool = cache_k.shape[0]
    ck = jnp.transpose(cache_k, (0, 2, 3, 1)).reshape(n_pool, KV_W, PAGE_SIZE)
    cv = jnp.transpose(cache_v, (0, 2, 3, 1)).reshape(n_pool, KV_W, PAGE_SIZE)
    pt_t = page_table.T
    q3 = qs.reshape(db, SB_HEADS, SB_HEAD_DIM)
    bias_rows = jnp.broadcast_to(bias.astype(F32).reshape(SB_HEADS, 1), (SB_HEADS, PAGE_SIZE))

    def page(m):
        return lambda b, s, pt: (pt[n_pages - 1 - (s * DEC_PPS + m), b], 0, 0)

    page_specs = [pl.BlockSpec((1, KV_W, PAGE_SIZE), page(m)) for m in range(DEC_PPS)]
    grid_spec = pltpu.PrefetchScalarGridSpec(
        num_scalar_prefetch=1,
        grid=(db, n_pages // DEC_PPS),
        in_specs=[pl.BlockSpec((1, SB_HEADS, SB_HEAD_DIM), lambda b, s, pt: (b, 0, 0)),
                  pl.BlockSpec((SB_HEADS, PAGE_SIZE), lambda b, s, pt: (0, 0))] + page_specs + page_specs,
        out_specs=pl.BlockSpec((1, SB_HEADS, SB_HEAD_DIM), lambda b, s, pt: (b, 0, 0)),
        scratch_shapes=[pltpu.VMEM((SB_HEADS, KV_W), F32),
                        pltpu.VMEM((SB_HEADS, 1), F32)],
    )
    out = pl.pallas_call(
        _attn_decode_kernel,
        out_shape=jax.ShapeDtypeStruct((db, SB_HEADS, SB_HEAD_DIM), F32),
        grid_spec=grid_spec,
        compiler_params=_params(("parallel", "arbitrary")),
        name="attn_decode",
    )(pt_t, q3, bias_rows, *([ck] * DEC_PPS), *([cv] * DEC_PPS))
    return out.reshape(db, SB_HEADS * SB_HEAD_DIM)


DEC_BT = 8


def _ssd_decode_kernel(xbc_ref, cprev_ref, dt_ref, z_ref, h0_ref, cw_ref, cb_ref, dtb_ref, alog_ref,
                       dskip_ref, nw_ref, yn_ref, hout_ref, cout_ref, ybuf):
    bt = DEC_BT
    x_new = xbc_ref[...]
    a = cb_ref[...]
    for tap in range(CONV_W - 1):
        a = a + cw_ref[tap:tap + 1, :] * cprev_ref[tap]
        if tap > 0:
            cout_ref[tap - 1] = cprev_ref[tap]
    a = a + cw_ref[CONV_W - 1:CONV_W, :] * x_new
    cout_ref[CONV_W - 2] = x_new
    xc = _silu(a)
    xs = xc[:, :D_INNER]
    bm = xc[:, D_INNER:D_INNER + SSM_GROUPS * D_STATE]
    cm = xc[:, D_INNER + SSM_GROUPS * D_STATE:]
    dtv = _softplus(dt_ref[...] + dtb_ref[...])
    da = jnp.exp(dtv * (-jnp.exp(alog_ref[...])))
    bmb = bm.astype(BF16)
    cmb = cm.astype(BF16)
    sub = lax.broadcasted_iota(jnp.int32, (bt, SSM_HEAD_DIM), 0)
    for g in range(SSM_GROUPS):
        gsl = slice(g * D_STATE, (g + 1) * D_STATE)
        bgb = bmb[:, gsl]
        cgb = cmb[:, gsl]
        cbg = jnp.sum(cgb.astype(F32) * bgb.astype(F32), axis=-1, keepdims=True)
        cbg = cbg.astype(BF16).astype(F32)
        for r in range(SSM_HPG):
            h = g * SSM_HPG + r
            hsl = slice(h * SSM_HEAD_DIM, (h + 1) * SSM_HEAD_DIM)
            xh = xs[:, hsl]
            xdt = xh * dtv[:, h:h + 1]
            xdtb = xdt.astype(BF16)
            y_diag = cbg * xdtb.astype(F32)
            for b in range(bt):
                h_in = h0_ref[b, h]
                y_off = _dot_nt(cgb, h_in.astype(BF16))
                x_one = jnp.where(sub == b, xdtb, jnp.zeros_like(xdtb))
                st = lax.dot_general(x_one, bgb, (((0,), (0,)), ((), ())),
                                     preferred_element_type=F32)
                hout_ref[b, h] = h_in * da[b:b + 1, h:h + 1] + st
                ybuf[b:b + 1, hsl] = y_off[b:b + 1, :] * da[b:b + 1, h:h + 1]
            ybuf[:, hsl] = (y_diag + ybuf[:, hsl]) + xh * dskip_ref[:, hsl]
    yn_ref[...] = _gated_rms_norm(ybuf[...], z_ref[...], nw_ref[...]).astype(BF16)


def _ssd_decode(xbc, conv_prev, dt, z, h0, conv_w, conv_b, dt_bias, a_log, d_skip, norm_w):
    db = xbc.shape[0]
    bt = DEC_BT
    row = lambda i: (i, 0)
    const = lambda i: (0, 0)
    out_shape = (
        jax.ShapeDtypeStruct((db, D_INNER), BF16),
        jax.ShapeDtypeStruct((db, SSM_HEADS, SSM_HEAD_DIM, D_STATE), F32),
        jax.ShapeDtypeStruct((CONV_W - 1, db, CONV_DIM), F32),
    )
    return pl.pallas_call(
        _ssd_decode_kernel,
        out_shape=out_shape,
        grid=(db // bt,),
        in_specs=[pl.BlockSpec((bt, CONV_DIM), row),
                  pl.BlockSpec((CONV_W - 1, bt, CONV_DIM), lambda i: (0, i, 0)),
                  pl.BlockSpec((bt, LANES), row),
                  pl.BlockSpec((bt, D_INNER), row),
                  pl.BlockSpec((bt, SSM_HEADS, SSM_HEAD_DIM, D_STATE), lambda i: (i, 0, 0, 0)),
                  pl.BlockSpec((CONV_W, CONV_DIM), const),
                  pl.BlockSpec((1, CONV_DIM), const),
                  pl.BlockSpec((1, LANES), const),
                  pl.BlockSpec((1, LANES), const),
                  pl.BlockSpec((1, D_INNER), const),
                  pl.BlockSpec((1, D_INNER), const)],
        out_specs=(pl.BlockSpec((bt, D_INNER), row),
                   pl.BlockSpec((bt, SSM_HEADS, SSM_HEAD_DIM, D_STATE), lambda i: (i, 0, 0, 0)),
                   pl.BlockSpec((CONV_W - 1, bt, CONV_DIM), lambda i: (0, i, 0))),
        scratch_shapes=[pltpu.VMEM((bt, D_INNER), F32)],
        compiler_params=_params(("parallel",)),
        name="ssd_decode",
    )(xbc, conv_prev, dt, z, h0, conv_w, conv_b.reshape(1, CONV_DIM), _pad_lanes(dt_bias),
      _pad_lanes(a_log), jnp.repeat(d_skip, SSM_HEAD_DIM).reshape(1, D_INNER),
      norm_w.reshape(1, D_INNER))


def _merge_kernel(yn_ref, at_ref, g_ref, x_ref, ws_ref, wa_ref, wo_ref, lg_ref, lb_ref,
                  x1_ref, x1b_ref):
    ssm_out = _dot(yn_ref[...], ws_ref[...])
    attn_out = _dot(at_ref[...].astype(BF16), wa_ref[...])
    mixed = _sigmoid(g_ref[:, :D_MODEL]) * ssm_out + _sigmoid(g_ref[:, D_MODEL:]) * attn_out
    r = DEEPNORM_ALPHA * x_ref[...] + _dot(mixed.astype(BF16), wo_ref[...])
    y = _layer_norm(r, lg_ref[...], lb_ref[...])
    x1_ref[...] = y
    x1b_ref[...] = y.astype(BF16)


def _resident(shape):
    return pl.BlockSpec(shape, lambda i: (0,) * len(shape), pipeline_mode=pl.Buffered(1))


def _merge(yn, attn, gates, x, ws, wa, wo, ln_g, ln_b, tm):
    t = x.shape[0]
    row = lambda i: (i, 0)
    return pl.pallas_call(
        _merge_kernel,
        out_shape=(jax.ShapeDtypeStruct((t, D_MODEL), F32), jax.ShapeDtypeStruct((t, D_MODEL), BF16)),
        grid=(t // tm,),
        in_specs=[pl.BlockSpec((tm, D_INNER), row),
                  pl.BlockSpec((tm, SB_HEADS * SB_HEAD_DIM), row),
                  pl.BlockSpec((tm, 2 * D_MODEL), row),
                  pl.BlockSpec((tm, D_MODEL), row),
                  _resident((D_INNER, D_MODEL)),
                  _resident((SB_HEADS * SB_HEAD_DIM, D_MODEL)),
                  _resident((D_MODEL, D_MODEL)),
                  _resident((1, D_MODEL)),
                  _resident((1, D_MODEL))],
        out_specs=(pl.BlockSpec((tm, D_MODEL), row), pl.BlockSpec((tm, D_MODEL), row)),
        compiler_params=_params(("parallel",)),
        name="merge_outproj_ln",
    )(yn, attn, gates, x, ws, wa, wo, ln_g.reshape(1, D_MODEL), ln_b.reshape(1, D_MODEL))


MLP_FC = 512


def _mlp_kernel(x1_ref, x1b_ref, wu_ref, wd_ref, lg_ref, lb_ref, y_ref):
    xb = x1b_ref[...]
    acc = jnp.zeros(x1_ref.shape, F32)
    for c in range(D_FF // MLP_FC):
        sl = slice(c * MLP_FC, (c + 1) * MLP_FC)
        hid = jnp.square(jnp.maximum(_dot(xb, wu_ref[:, sl]), 0.0))
        acc = acc + _dot(hid.astype(BF16), wd_ref[sl, :])
    r = DEEPNORM_ALPHA * x1_ref[...] + acc
    y_ref[...] = _layer_norm(r, lg_ref[...], lb_ref[...])


def _mlp(x1, x1b, wu, wd, ln_g, ln_b, tm):
    t = x1.shape[0]
    row = lambda i: (i, 0)
    return pl.pallas_call(
        _mlp_kernel,
        out_shape=jax.ShapeDtypeStruct((t, D_MODEL), F32),
        grid=(t // tm,),
        in_specs=[pl.BlockSpec((tm, D_MODEL), row),
                  pl.BlockSpec((tm, D_MODEL), row),
                  _resident((D_MODEL, D_FF)),
                  _resident((D_FF, D_MODEL)),
                  _resident((1, D_MODEL)),
                  _resident((1, D_MODEL))],
        out_specs=pl.BlockSpec((tm, D_MODEL), row),
        compiler_params=_params(("parallel",)),
        name="mlp_ln",
    )(x1, x1b, wu, wd, ln_g.reshape(1, D_MODEL), ln_b.reshape(1, D_MODEL))


def _row_tile(t, want):
    tm = min(t, want)
    assert t % tm == 0, (t, tm)
    return tm


def kernel(x_prompt, x_sample, cache_k, cache_v, state_ssm, state_conv, page_table, w_in, conv_w, conv_b,
           dt_bias, a_log, d_skip, ssm_norm_w, w_ssm_br, w_attn_br, sb_logit_bias, w_out, ln1_g, ln1_b,
           w_up, w_down, ln2_g, ln2_b):
    depth = w_in.shape[0]
    assert depth == 1, "single-layer step"
    bsz, seqlen, _ = x_prompt.shape
    db, dseq, _ = x_sample.shape
    assert dseq == 1 and seqlen % ATT_KB == 0 and page_table.shape[1] % DEC_PPS == 0 and db % DEC_BT == 0
    tp = bsz * seqlen

    w_r, wkv_t = _prep_w_in(w_in[0])
    ws = w_ssm_br[0].astype(BF16)
    wa = w_attn_br[0].astype(BF16)
    wo = w_out[0].astype(BF16)
    wu = w_up[0].astype(BF16)
    wd = w_down[0].astype(BF16)
    ssm_p = (conv_w[0], conv_b[0], dt_bias[0], a_log[0], d_skip[0], ssm_norm_w[0])

    def trunk_tail(yn, attn, gates, x, tm):
        x1, x1b = _merge(yn, attn, gates, x, ws, wa, wo, ln1_g[0], ln1_b[0], tm)
        return _mlp(x1, x1b, wu, wd, ln2_g[0], ln2_b[0], tm)

    xp = x_prompt.reshape(tp, D_MODEL)
    z, xbc, qs, gates, dt, kt, vt, kb, vb = _in_projection(
        xp, w_r, wkv_t, bsz, seqlen, _row_tile(seqlen, 1024), True)
    yn, ssm_prompt, conv_prompt = _ssd_prompt(xbc, dt, z, *ssm_p, bsz, seqlen)
    attn = _attn_prompt(qs, kb, vb, sb_logit_bias[0], bsz, seqlen)
    y_prompt = trunk_tail(yn, attn, gates, xp, _row_tile(tp, 512)).reshape(bsz, seqlen, D_MODEL)

    xs_ = x_sample.reshape(db, D_MODEL)
    z, xbc, qs, gates, dt, kts, vts = _in_projection(xs_, w_r, wkv_t, 1, db, db, False)
    conv_prev = jnp.transpose(state_conv[0], (1, 0, 2))
    yn, ssm_sample, conv_sample = _ssd_decode(xbc, conv_prev, dt, z, state_ssm[0], *ssm_p)
    attn = _attn_decode(qs, cache_k[0], cache_v[0], page_table, sb_logit_bias[0])
    y_sample = trunk_tail(yn, attn, gates, xs_, db).reshape(db, 1, D_MODEL)

    def token_major(xt, shape):
        n, _, s = xt.shape
        x = xt.reshape(n, SB_KV_HEADS, SB_HEAD_DIM, s)
        return jnp.transpose(x, (0, 3, 1, 2)).reshape(shape)

    kv_shape_p = (1, bsz, seqlen, SB_KV_HEADS, SB_HEAD_DIM)
    kv_shape_s = (1, db, 1, SB_KV_HEADS, SB_HEAD_DIM)
    return (y_prompt, y_sample, token_major(kt, kv_shape_p), token_major(vt, kv_shape_p),
            ssm_prompt[None], conv_prompt[None], token_major(kts, kv_shape_s),
            token_major(vts, kv_shape_s), ssm_sample[None],
            jnp.transpose(conv_sample, (1, 0, 2))[None])
```

```python
import functools

import jax
import jax.numpy as jnp
from jax import lax
from jax.experimental import pallas as pl
from jax.experimental.pallas import tpu as pltpu

F32 = jnp.float32
BF16 = jnp.bfloat16

D_MODEL = 1024
D_INNER = 2048
SSM_HEAD_DIM = 64
SSM_HEADS = 32
SSM_GROUPS = 4
SSM_HPG = SSM_HEADS // SSM_GROUPS
D_STATE = 128
CONV_W = 4
CONV_DIM = D_INNER + 2 * SSM_GROUPS * D_STATE
SSD_CHUNK = 128
SB_HEAD_DIM = 64
SB_HEADS = 16
SB_KV_HEADS = 4
SB_REP = SB_HEADS // SB_KV_HEADS
SB_SCALE = SB_HEAD_DIM ** -0.5
D_FF = 4 * D_MODEL
PAGE_SIZE = 128
DEEPNORM_ALPHA = 2.0 ** 0.25
LN_EPS = 1e-5
RMS_EPS = 1e-5
IN_SIZES = (D_INNER, CONV_DIM, SSM_HEADS, SB_HEADS * SB_HEAD_DIM,
            SB_KV_HEADS * SB_HEAD_DIM, SB_KV_HEADS * SB_HEAD_DIM, 2 * D_MODEL)

LANES = 128
VMEM_LIMIT_BYTES = 56 * 1024 * 1024

PROJ_TN = 512
_TZ = D_INNER // PROJ_TN
_TX = CONV_DIM // PROJ_TN
_TQ = (SB_HEADS * SB_HEAD_DIM) // PROJ_TN
_TG = (2 * D_MODEL) // PROJ_TN
_J_XBC = _TZ
_J_Q = _J_XBC + _TX
_J_G = _J_Q + _TQ
_J_DT = _J_G + _TG
_J_KV = _J_DT + 1
PROJ_TILES = _J_KV + 1
KV_W = SB_KV_HEADS * SB_HEAD_DIM

ATT_QB = 256
ATT_KB = 256
DEC_PPS = 32

_NT = (((1,), (1,)), ((), ()))


def _dot(a, b):
    return jnp.dot(a, b, preferred_element_type=F32)


def _dot_nt(a, b):
    return lax.dot_general(a, b, _NT, preferred_element_type=F32)


def _sigmoid(x):
    return 1.0 / (1.0 + jnp.exp(-x))


def _silu(x):
    return x * _sigmoid(x)


def _layer_norm(r, g, b):
    mu = jnp.mean(r, axis=-1, keepdims=True)
    d = r - mu
    var = jnp.mean(d * d, axis=-1, keepdims=True)
    return d * lax.rsqrt(var + LN_EPS) * g + b


def _split3(x):
    hi = x.astype(BF16)
    r = x - hi.astype(F32)
    mid = r.astype(BF16)
    lo = (r - mid.astype(F32)).astype(BF16)
    return hi, mid, lo


def _params(sem):
    return pltpu.CompilerParams(dimension_semantics=sem, vmem_limit_bytes=VMEM_LIMIT_BYTES)


def _inproj_kernel(x_ref, w_ref, wkv_ref, *refs, key_blocks):
    if key_blocks:
        z_ref, xbc_ref, q_ref, g_ref, dt_ref, kt_ref, vt_ref, kb_ref, vb_ref, xb_ref = refs
    else:
        z_ref, xbc_ref, q_ref, g_ref, dt_ref, kt_ref, vt_ref, xb_ref = refs
    j = pl.program_id(1)

    @pl.when(j == 0)
    def _():
        xb_ref[...] = x_ref[...].astype(BF16)

    def mm():
        return _dot(xb_ref[...], w_ref[...])

    @pl.when(j < _J_XBC)
    def _():
        z_ref[...] = mm()

    @pl.when((j >= _J_XBC) & (j < _J_Q))
    def _():
        xbc_ref[...] = mm()

    @pl.when((j >= _J_Q) & (j < _J_G))
    def _():
        q_ref[...] = (mm() * SB_SCALE).astype(BF16)

    @pl.when((j >= _J_G) & (j < _J_DT))
    def _():
        g_ref[...] = mm()

    @pl.when(j == _J_DT)
    def _():
        dt_ref[...] = mm()[:, :LANES]

    @pl.when(j == _J_KV)
    def _():
        kvt = _dot_nt(wkv_ref[...], xb_ref[...])
        kt_ref[0] = kvt[:KV_W]
        vt_ref[0] = kvt[KV_W:]
        if key_blocks:
            for c in range(key_blocks):
                csl = slice(c * ATT_KB, (c + 1) * ATT_KB)
                for g in range(SB_KV_HEADS):
                    kb_ref[0, c, g] = kvt[g * SB_HEAD_DIM:(g + 1) * SB_HEAD_DIM, csl].astype(BF16)
                    vb_ref[0, c, g] = kvt[KV_W + g * SB_HEAD_DIM:KV_W + (g + 1) * SB_HEAD_DIM, csl].astype(BF16)


def _prep_w_in(w_in):
    splits = []
    off = 0
    for s in IN_SIZES:
        splits.append(w_in[:, off:off + s])
        off += s
    wz, wxbc, wdt, wq, wk, wv, wg = splits
    wdt = jnp.pad(wdt, ((0, 0), (0, PROJ_TN - SSM_HEADS)))
    w_r = jnp.concatenate([wz, wxbc, wq, wg, wdt], axis=1).astype(BF16)
    wkv_t = jnp.concatenate([wk, wv], axis=1).T.astype(BF16)
    return w_r, wkv_t


def _in_projection(x, w_r, wkv_t, bsz, seqlen, tm, with_key_blocks):
    t = x.shape[0]
    tpb = seqlen // tm
    key_blocks = tm // ATT_KB if with_key_blocks else 0

    def tile(lo, n):
        return lambda i, j: (i, jnp.clip(j - lo, 0, n - 1))

    row = lambda i, j: (i, 0)
    seq3 = lambda i, j: (i // tpb, 0, i % tpb)
    seq5 = lambda i, j: (i // tpb, i % tpb, 0, 0, 0)
    out_shape = [
        jax.ShapeDtypeStruct((t, D_INNER), F32),
        jax.ShapeDtypeStruct((t, CONV_DIM), F32),
        jax.ShapeDtypeStruct((t, SB_HEADS * SB_HEAD_DIM), BF16),
        jax.ShapeDtypeStruct((t, 2 * D_MODEL), F32),
        jax.ShapeDtypeStruct((t, LANES), F32),
        jax.ShapeDtypeStruct((bsz, KV_W, seqlen), F32),
        jax.ShapeDtypeStruct((bsz, KV_W, seqlen), F32),
    ]
    out_specs = [
        pl.BlockSpec((tm, PROJ_TN), tile(0, _TZ)),
        pl.BlockSpec((tm, PROJ_TN), tile(_J_XBC, _TX)),
        pl.BlockSpec((tm, PROJ_TN), tile(_J_Q, _TQ)),
        pl.BlockSpec((tm, PROJ_TN), tile(_J_G, _TG)),
        pl.BlockSpec((tm, LANES), row),
        pl.BlockSpec((1, KV_W, tm), seq3),
        pl.BlockSpec((1, KV_W, tm), seq3),
    ]
    if key_blocks:
        blk = (bsz, seqlen // ATT_KB, SB_KV_HEADS, SB_HEAD_DIM, ATT_KB)
        out_shape += [jax.ShapeDtypeStruct(blk, BF16)] * 2
        out_specs += [pl.BlockSpec((1, key_blocks, SB_KV_HEADS, SB_HEAD_DIM, ATT_KB), seq5)] * 2
    return pl.pallas_call(
        functools.partial(_inproj_kernel, key_blocks=key_blocks),
        out_shape=tuple(out_shape),
        grid=(t // tm, PROJ_TILES),
        in_specs=[pl.BlockSpec((tm, D_MODEL), row),
                  pl.BlockSpec((D_MODEL, PROJ_TN), lambda i, j: (0, jnp.minimum(j, _J_KV - 1))),
                  pl.BlockSpec((2 * KV_W, D_MODEL), lambda i, j: (0, 0))],
        out_specs=tuple(out_specs),
        scratch_shapes=[pltpu.VMEM((tm, D_MODEL), BF16)],
        compiler_params=_params(("parallel", "arbitrary")),
        name="in_projection",
    )(x, w_r, wkv_t)


def _softplus(x):
    return jnp.maximum(x, 0.0) + jnp.log1p(jnp.exp(-jnp.abs(x)))


def _gated_rms_norm(y, z, w):
    h = y * _silu(z)
    h = h * lax.rsqrt(jnp.mean(h * h, axis=-1, keepdims=True) + RMS_EPS)
    return h * w


def _head_lanes(x, sel):
    pieces = [p[:, :SSM_HEADS] for p in _split3(x)]
    pad = jnp.zeros((x.shape[0], LANES - 3 * SSM_HEADS), BF16)
    return _dot(jnp.concatenate(pieces + [pad], axis=1), sel)


def _head_selector(lanes_per_head):
    k = jnp.arange(LANES)[:, None]
    n = jnp.arange(SSM_HEADS * lanes_per_head)[None, :]
    return ((k % SSM_HEADS == n // lanes_per_head) & (k < 3 * SSM_HEADS)).astype(BF16)


def _ssd_prompt_kernel(xbc_ref, dt_ref, z_ref, cw_ref, cb_ref, dtb_ref, alog_ref, dskip_ref, nw_ref,
                       selp_ref, sels_ref, yn_ref, hout_ref, cout_ref,
                       xwin, xc, hst, ybuf, dtx_s, ecx_s, tex_s, colb_s):
    c = pl.program_id(1)
    nc = pl.num_programs(1)
    q = SSD_CHUNK
    tail = CONV_W - 1

    @pl.when(c == 0)
    def _():
        xwin[0:8, :] = jnp.zeros((8, CONV_DIM), F32)
        hst[...] = jnp.zeros(hst.shape, F32)

    xwin[8:8 + q, :] = xbc_ref[...]
    cw = 512
    for j in range(CONV_DIM // cw):
        sl = slice(j * cw, (j + 1) * cw)
        a = cb_ref[:, sl] + cw_ref[0:1, sl] * xwin[8 - tail:8 - tail + q, sl]
        for tap in range(1, CONV_W):
            a = a + cw_ref[tap:tap + 1, sl] * xwin[8 - tail + tap:8 - tail + tap + q, sl]
        xc[:, sl] = _silu(a)

    @pl.when(c == nc - 1)
    def _():
        cout_ref[0] = xwin[8 + q - tail:8 + q, :]

    xwin[0:8, :] = xwin[q:q + 8, :]

    dtv = _softplus(dt_ref[...] + dtb_ref[...])
    neg_a = -jnp.exp(alog_ref[...])
    da = dtv * neg_a
    rows = lax.broadcasted_iota(jnp.int32, (q, q), 0)
    cols = lax.broadcasted_iota(jnp.int32, (q, q), 1)
    causal = rows >= cols
    tri = causal.astype(BF16)
    hi, mid, lo = _split3(da)
    cum = (_dot(tri, lo) + _dot(tri, mid)) + _dot(tri, hi)
    cum_t = cum.T
    cumx = _head_lanes(cum, selp_ref[...])
    cum_last_x = cumx[q - 1:q, :]
    dtx_s[...] = _head_lanes(dtv, selp_ref[...])
    ecx_s[...] = jnp.exp(cumx)
    tex_s[...] = jnp.exp(cum_last_x - cumx)
    cdec_x = jnp.exp(cum_last_x)
    colb_s[...] = _head_lanes(cum, sels_ref[...])

    first_head = lax.broadcasted_iota(jnp.int32, (q, 2 * SSM_HEAD_DIM), 1) < SSM_HEAD_DIM
    bm_off = D_INNER
    cm_off = D_INNER + SSM_GROUPS * D_STATE
    pairs_per_group = SSM_HPG // 2
    for g in range(SSM_GROUPS):
        bg = xc[:, bm_off + g * D_STATE:bm_off + (g + 1) * D_STATE]
        cg = xc[:, cm_off + g * D_STATE:cm_off + (g + 1) * D_STATE]
        bgb = bg.astype(BF16)
        cgb = cg.astype(BF16)
        cb = _dot_nt(cgb, bgb)
        bgtb = bg.T.astype(BF16)
        for r in range(pairs_per_group):
            pair = g * pairs_per_group + r
            psl = slice(pair * 2 * SSM_HEAD_DIM, (pair + 1) * 2 * SSM_HEAD_DIM)
            ms = []
            for h in (2 * pair, 2 * pair + 1):
                seg = colb_s[:, h * q:(h + 1) * q] - cum_t[h:h + 1, :]
                decay = jnp.exp(jnp.where(causal, seg, -jnp.inf))
                ms.append((decay * cb).astype(BF16))
            xp = xc[:, psl]
            xdt = xp * dtx_s[:, psl]
            xdtb = xdt.astype(BF16)
            zero = jnp.zeros_like(xdtb)
            rhs = jnp.concatenate([jnp.where(first_head, xdtb, zero), jnp.where(first_head, zero, xdtb)],
                                  axis=0)
            y_diag = _dot(jnp.concatenate(ms, axis=1), rhs)
            h_in = hst[pair]
            y_off = _dot(cgb, h_in.astype(BF16)) * ecx_s[:, psl]
            st = _dot(bgtb, (xdt * tex_s[:, psl]).astype(BF16))
            hst[pair] = h_in * cdec_x[:, psl] + st
            ybuf[:, psl] = (y_diag + y_off) + xp * dskip_ref[:, psl]

    yn_ref[...] = _gated_rms_norm(ybuf[...], z_ref[...], nw_ref[...]).astype(BF16)

    @pl.when(c == nc - 1)
    def _():
        for pair in range(SSM_HEADS // 2):
            ht = hst[pair].T
            hout_ref[0, 2 * pair] = ht[:SSM_HEAD_DIM]
            hout_ref[0, 2 * pair + 1] = ht[SSM_HEAD_DIM:]


def _pad_lanes(v):
    return jnp.pad(v, (0, LANES - v.shape[0])).reshape(1, LANES)


def _ssd_prompt(xbc, dt, z, conv_w, conv_b, dt_bias, a_log, d_skip, norm_w, bsz, seqlen):
    t = bsz * seqlen
    q = SSD_CHUNK
    nc = seqlen // q
    tok = lambda b, c: (b * nc + c, 0)
    const = lambda b, c: (0, 0)
    out_shape = (
        jax.ShapeDtypeStruct((t, D_INNER), BF16),
        jax.ShapeDtypeStruct((bsz, SSM_HEADS, SSM_HEAD_DIM, D_STATE), F32),
        jax.ShapeDtypeStruct((bsz, CONV_W - 1, CONV_DIM), F32),
    )
    return pl.pallas_call(
        _ssd_prompt_kernel,
        out_shape=out_shape,
        grid=(bsz, nc),
        in_specs=[pl.BlockSpec((q, CONV_DIM), tok),
                  pl.BlockSpec((q, LANES), tok),
                  pl.BlockSpec((q, D_INNER), tok),
                  pl.BlockSpec((CONV_W, CONV_DIM), const),
                  pl.BlockSpec((1, CONV_DIM), const),
                  pl.BlockSpec((1, LANES), const),
                  pl.BlockSpec((1, LANES), const),
                  pl.BlockSpec((1, D_INNER), const),
                  pl.BlockSpec((1, D_INNER), const),
                  pl.BlockSpec((LANES, D_INNER), const),
                  pl.BlockSpec((LANES, SSM_HEADS * q), const)],
        out_specs=(pl.BlockSpec((q, D_INNER), tok),
                   pl.BlockSpec((1, SSM_HEADS, SSM_HEAD_DIM, D_STATE), lambda b, c: (b, 0, 0, 0)),
                   pl.BlockSpec((1, CONV_W - 1, CONV_DIM), lambda b, c: (b, 0, 0))),
        scratch_shapes=[pltpu.VMEM((q + 8, CONV_DIM), F32),
                        pltpu.VMEM((q, CONV_DIM), F32),
                        pltpu.VMEM((SSM_HEADS // 2, D_STATE, 2 * SSM_HEAD_DIM), F32),
                        pltpu.VMEM((q, D_INNER), F32),
                        pltpu.VMEM((q, D_INNER), F32),
                        pltpu.VMEM((q, D_INNER), F32),
                        pltpu.VMEM((q, D_INNER), F32),
                        pltpu.VMEM((q, SSM_HEADS * q), F32)],
        compiler_params=_params(("parallel", "arbitrary")),
        name="ssd_prompt",
    )(xbc, dt, z, conv_w, conv_b.reshape(1, CONV_DIM), _pad_lanes(dt_bias), _pad_lanes(a_log),
      jnp.repeat(d_skip, SSM_HEAD_DIM).reshape(1, D_INNER), norm_w.reshape(1, D_INNER),
      _head_selector(SSM_HEAD_DIM), _head_selector(q))


LOG2E = 1.4426950408889634


def _sb_weights(zs, keeps, tmat, carry, stacked):
    rows = zs[0].shape[0]
    sps = []
    for z, keep in zip(zs, keeps):
        sp = jnp.maximum(z, 0.0) + jnp.log(1.0 + jnp.exp2(jnp.abs(z) * (-LOG2E)))
        sps.append(sp if keep is None else sp * keep)
    if stacked:
        cum = _dot(jnp.concatenate([sp.astype(BF16) for sp in sps], axis=0), tmat)
        cums = [cum[n * rows:(n + 1) * rows] for n in range(len(zs))]
    else:
        cums = [_dot(sp.astype(BF16), tmat) for sp in sps]
    ws = []
    for z, sp, c, keep in zip(zs, sps, cums, keeps):
        w = jnp.exp(z - sp - c - carry)
        ws.append(w if keep is None else w * keep)
        carry = carry + (c[:, 0:1] + sp[:, 0:1])
    return ws, carry


def _later_key_matrix(n):
    r = jnp.arange(n)[:, None]
    c = jnp.arange(n)[None, :]
    return (r > c).astype(BF16)


def _attn_prompt_kernel(q_ref, k_ref, v_ref, baug_ref, keep_ref, tmat_ref, ones_ref, o_ref,
                        acc_ref, carry_ref):
    i = pl.program_id(2)
    qb = q_ref[...]
    qs = jnp.concatenate([qb[:, r * SB_HEAD_DIM:(r + 1) * SB_HEAD_DIM] for r in range(SB_REP)], axis=0)
    q_aug = jnp.concatenate([qs, baug_ref[0]], axis=1)
    tmat = tmat_ref[...]

    def blocks(jbs, keeps):
        ones = ones_ref[...]
        zs = [_dot(q_aug, jnp.concatenate([k_ref[0, jb, 0], ones], axis=0)) for jb in jbs]
        ws, carry = _sb_weights(zs, keeps, tmat, carry_ref[...], False)
        pv = None
        for jb, w in zip(jbs, ws):
            pv_b = _dot_nt(w.astype(BF16), v_ref[0, jb, 0])
            pv = pv_b if pv is None else pv + pv_b
        acc_ref[...] += pv
        carry_ref[...] = carry

    acc_ref[...] = jnp.zeros(acc_ref.shape, F32)
    carry_ref[...] = jnp.zeros(carry_ref.shape, F32)
    per = ATT_KB // ATT_QB
    jd = i // per
    blocks([jd], [keep_ref[i - jd * per]])

    @pl.when(jd % 2 == 1)
    def _():
        blocks([jd - 1], [None])

    npair = jd // 2

    def body(t, _):
        hi = 2 * (npair - t) - 1
        blocks([hi, hi - 1], [None, None])
        return 0

    lax.fori_loop(0, npair, body, 0)

    for r in range(SB_REP):
        o_ref[:, r * SB_HEAD_DIM:(r + 1) * SB_HEAD_DIM] = acc_ref[r * ATT_QB:(r + 1) * ATT_QB, :].astype(BF16)


def _attn_prompt(qs, kb, vb, bias, bsz, seqlen):
    t = bsz * seqlen
    nq = seqlen // ATT_QB
    nkb = seqlen // ATT_KB
    rows = SB_REP * ATT_QB
    gw = SB_REP * SB_HEAD_DIM
    parts = jnp.stack(_split3(bias.astype(F32)), axis=-1)
    parts = jnp.pad(parts, ((0, 0), (0, SB_HEAD_DIM - 3)))
    bias_aug = jnp.broadcast_to(parts.reshape(SB_KV_HEADS, SB_REP, 1, SB_HEAD_DIM),
                                (SB_KV_HEADS, SB_REP, ATT_QB, SB_HEAD_DIM)).reshape(
                                    SB_KV_HEADS, rows, SB_HEAD_DIM)
    ones_rows = jnp.broadcast_to((jnp.arange(SB_HEAD_DIM) < 3).astype(BF16)[:, None],
                                 (SB_HEAD_DIM, ATT_KB))
    qpos = jnp.arange(rows) % ATT_QB
    keep = jnp.stack([(jnp.arange(ATT_KB)[None, :] < (par * ATT_QB + qpos)[:, None]).astype(F32)
                      for par in range(ATT_KB // ATT_QB)])
    const2 = lambda b, g, i: (0, 0)
    return pl.pallas_call(
        _attn_prompt_kernel,
        out_shape=jax.ShapeDtypeStruct((t, SB_HEADS * SB_HEAD_DIM), BF16),
        grid=(bsz, SB_KV_HEADS, nq),
        in_specs=[pl.BlockSpec((ATT_QB, gw), lambda b, g, i: (b * nq + i, g)),
                  pl.BlockSpec((1, nkb, 1, SB_HEAD_DIM, ATT_KB), lambda b, g, i: (b, 0, g, 0, 0)),
                  pl.BlockSpec((1, nkb, 1, SB_HEAD_DIM, ATT_KB), lambda b, g, i: (b, 0, g, 0, 0)),
                  pl.BlockSpec((1, rows, SB_HEAD_DIM), lambda b, g, i: (g, 0, 0)),
                  pl.BlockSpec((ATT_KB // ATT_QB, rows, ATT_KB), lambda b, g, i: (0, 0, 0)),
                  pl.BlockSpec((ATT_KB, ATT_KB), const2),
                  pl.BlockSpec((SB_HEAD_DIM, ATT_KB), const2)],
        out_specs=pl.BlockSpec((ATT_QB, gw), lambda b, g, i: (b * nq + i, g)),
        scratch_shapes=[pltpu.VMEM((rows, SB_HEAD_DIM), F32),
                        pltpu.VMEM((rows, 1), F32)],
        compiler_params=_params(("parallel", "parallel", "arbitrary")),
        name="attn_prompt",
    )(qs, kb, vb, bias_aug, keep, _later_key_matrix(ATT_KB), ones_rows)


def _attn_decode_kernel(pt_ref, q_ref, bias_ref, tmat_ref, *refs):
    k_refs = refs[:DEC_PPS]
    v_refs = refs[DEC_PPS:2 * DEC_PPS]
    o_ref, acc_ref, carry_ref = refs[2 * DEC_PPS:]
    s = pl.program_id(1)

    @pl.when(s == 0)
    def _():
        acc_ref[...] = jnp.zeros(acc_ref.shape, F32)
        carry_ref[...] = jnp.zeros(carry_ref.shape, F32)

    hq = lax.broadcasted_iota(jnp.int32, (SB_HEADS, KV_W), 0) // SB_REP
    gl = lax.broadcasted_iota(jnp.int32, (SB_HEADS, KV_W), 1) // SB_HEAD_DIM
    own = hq == gl
    q_t = jnp.concatenate([q_ref[0]] * SB_KV_HEADS, axis=1)
    q_bd = jnp.where(own, q_t, jnp.zeros_like(q_t))
    bias = bias_ref[...]
    zs = [_dot(q_bd, k_refs[m][0].astype(BF16)) + bias for m in range(DEC_PPS)]
    ws, carry = _sb_weights(zs, [None] * DEC_PPS, tmat_ref[...], carry_ref[...], True)
    w = jnp.concatenate([w.astype(BF16) for w in ws], axis=1)
    vt = jnp.concatenate([v_refs[m][0].astype(BF16) for m in range(DEC_PPS)], axis=1)
    acc_ref[...] += _dot_nt(w, vt)
    carry_ref[...] = carry

    @pl.when(s == pl.num_programs(1) - 1)
    def _():
        acc = jnp.where(own, acc_ref[...], 0.0)
        out = acc[:, :SB_HEAD_DIM]
        for g in range(1, SB_KV_HEADS):
            out = out + acc[:, g * SB_HEAD_DIM:(g + 1) * SB_HEAD_DIM]
        o_ref[0] = out


def _attn_decode(qs, cache_k, cache_v, page_table, bias):
    db, n_pages = page_table.shape
    n_pool = cache_k.shape[0]
    ck = jnp.transpose(cache_k, (0, 2, 3, 1)).reshape(n_pool, KV_W, PAGE_SIZE)
    cv = jnp.transpose(cache_v, (0, 2, 3, 1)).reshape(n_pool, KV_W, PAGE_SIZE)
    pt_t = page_table.T
    q3 = qs.reshape(db, SB_HEADS, SB_HEAD_DIM)
    bias_rows = jnp.broadcast_to(bias.astype(F32).reshape(SB_HEADS, 1), (SB_HEADS, PAGE_SIZE))

    def page(m):
        return lambda b, s, pt: (pt[n_pages - 1 - (s * DEC_PPS + m), b], 0, 0)

    page_specs = [pl.BlockSpec((1, KV_W, PAGE_SIZE), page(m)) for m in range(DEC_PPS)]
    grid_spec = pltpu.PrefetchScalarGridSpec(
        num_scalar_prefetch=1,
        grid=(db, n_pages // DEC_PPS),
        in_specs=[pl.BlockSpec((1, SB_HEADS, SB_HEAD_DIM), lambda b, s, pt: (b, 0, 0)),
                  pl.BlockSpec((SB_HEADS, PAGE_SIZE), lambda b, s, pt: (0, 0)),
                  pl.BlockSpec((PAGE_SIZE, PAGE_SIZE), lambda b, s, pt: (0, 0))] + page_specs + page_specs,
        out_specs=pl.BlockSpec((1, SB_HEADS, SB_HEAD_DIM), lambda b, s, pt: (b, 0, 0)),
        scratch_shapes=[pltpu.VMEM((SB_HEADS, KV_W), F32),
                        pltpu.VMEM((SB_HEADS, 1), F32)],
    )
    out = pl.pallas_call(
        _attn_decode_kernel,
        out_shape=jax.ShapeDtypeStruct((db, SB_HEADS, SB_HEAD_DIM), F32),
        grid_spec=grid_spec,
        compiler_params=_params(("parallel", "arbitrary")),
        name="attn_decode",
    )(pt_t, q3, bias_rows, _later_key_matrix(PAGE_SIZE), *([ck] * DEC_PPS), *([cv] * DEC_PPS))
    return out.reshape(db, SB_HEADS * SB_HEAD_DIM)


DEC_BT = 8


def _ssd_decode_kernel(xbc_ref, cprev_ref, dt_ref, z_ref, h0_ref, cw_ref, cb_ref, dtb_ref, alog_ref,
                       dskip_ref, nw_ref, yn_ref, hout_ref, cout_ref, ybuf):
    bt = DEC_BT
    x_new = xbc_ref[...]
    a = cb_ref[...]
    for tap in range(CONV_W - 1):
        a = a + cw_ref[tap:tap + 1, :] * cprev_ref[tap]
        if tap > 0:
            cout_ref[tap - 1] = cprev_ref[tap]
    a = a + cw_ref[CONV_W - 1:CONV_W, :] * x_new
    cout_ref[CONV_W - 2] = x_new
    xc = _silu(a)
    xs = xc[:, :D_INNER]
    bm = xc[:, D_INNER:D_INNER + SSM_GROUPS * D_STATE]
    cm = xc[:, D_INNER + SSM_GROUPS * D_STATE:]
    dtv = _softplus(dt_ref[...] + dtb_ref[...])
    da = jnp.exp(dtv * (-jnp.exp(alog_ref[...])))
    bmb = bm.astype(BF16)
    cmb = cm.astype(BF16)
    sub = lax.broadcasted_iota(jnp.int32, (bt, SSM_HEAD_DIM), 0)
    for g in range(SSM_GROUPS):
        gsl = slice(g * D_STATE, (g + 1) * D_STATE)
        bgb = bmb[:, gsl]
        cgb = cmb[:, gsl]
        cbg = jnp.sum(cgb.astype(F32) * bgb.astype(F32), axis=-1, keepdims=True)
        cbg = cbg.astype(BF16).astype(F32)
        for r in range(SSM_HPG):
            h = g * SSM_HPG + r
            hsl = slice(h * SSM_HEAD_DIM, (h + 1) * SSM_HEAD_DIM)
            xh = xs[:, hsl]
            xdt = xh * dtv[:, h:h + 1]
            xdtb = xdt.astype(BF16)
            y_diag = cbg * xdtb.astype(F32)
            for b in range(bt):
                h_in = h0_ref[b, h]
                y_off = _dot_nt(cgb, h_in.astype(BF16))
                x_one = jnp.where(sub == b, xdtb, jnp.zeros_like(xdtb))
                st = lax.dot_general(x_one, bgb, (((0,), (0,)), ((), ())),
                                     preferred_element_type=F32)
                hout_ref[b, h] = h_in * da[b:b + 1, h:h + 1] + st
                ybuf[b:b + 1, hsl] = y_off[b:b + 1, :] * da[b:b + 1, h:h + 1]
            ybuf[:, hsl] = (y_diag + ybuf[:, hsl]) + xh * dskip_ref[:, hsl]
    yn_ref[...] = _gated_rms_norm(ybuf[...], z_ref[...], nw_ref[...]).astype(BF16)


def _ssd_decode(xbc, conv_prev, dt, z, h0, conv_w, conv_b, dt_bias, a_log, d_skip, norm_w):
    db = xbc.shape[0]
    bt = DEC_BT
    row = lambda i: (i, 0)
    const = lambda i: (0, 0)
    out_shape = (
        jax.ShapeDtypeStruct((db, D_INNER), BF16),
        jax.ShapeDtypeStruct((db, SSM_HEADS, SSM_HEAD_DIM, D_STATE), F32),
        jax.ShapeDtypeStruct((CONV_W - 1, db, CONV_DIM), F32),
    )
    return pl.pallas_call(
        _ssd_decode_kernel,
        out_shape=out_shape,
        grid=(db // bt,),
        in_specs=[pl.BlockSpec((bt, CONV_DIM), row),
                  pl.BlockSpec((CONV_W - 1, bt, CONV_DIM), lambda i: (0, i, 0)),
                  pl.BlockSpec((bt, LANES), row),
                  pl.BlockSpec((bt, D_INNER), row),
                  pl.BlockSpec((bt, SSM_HEADS, SSM_HEAD_DIM, D_STATE), lambda i: (i, 0, 0, 0)),
                  pl.BlockSpec((CONV_W, CONV_DIM), const),
                  pl.BlockSpec((1, CONV_DIM), const),
                  pl.BlockSpec((1, LANES), const),
                  pl.BlockSpec((1, LANES), const),
                  pl.BlockSpec((1, D_INNER), const),
                  pl.BlockSpec((1, D_INNER), const)],
        out_specs=(pl.BlockSpec((bt, D_INNER), row),
                   pl.BlockSpec((bt, SSM_HEADS, SSM_HEAD_DIM, D_STATE), lambda i: (i, 0, 0, 0)),
                   pl.BlockSpec((CONV_W - 1, bt, CONV_DIM), lambda i: (0, i, 0))),
        scratch_shapes=[pltpu.VMEM((bt, D_INNER), F32)],
        compiler_params=_params(("parallel",)),
        name="ssd_decode",
    )(xbc, conv_prev, dt, z, h0, conv_w, conv_b.reshape(1, CONV_DIM), _pad_lanes(dt_bias),
      _pad_lanes(a_log), jnp.repeat(d_skip, SSM_HEAD_DIM).reshape(1, D_INNER),
      norm_w.reshape(1, D_INNER))


def _merge_kernel(yn_ref, at_ref, g_ref, x_ref, ws_ref, wa_ref, wo_ref, lg_ref, lb_ref,
                  x1_ref, x1b_ref):
    ssm_out = _dot(yn_ref[...], ws_ref[...])
    attn_out = _dot(at_ref[...].astype(BF16), wa_ref[...])
    mixed = _sigmoid(g_ref[:, :D_MODEL]) * ssm_out + _sigmoid(g_ref[:, D_MODEL:]) * attn_out
    r = DEEPNORM_ALPHA * x_ref[...] + _dot(mixed.astype(BF16), wo_ref[...])
    y = _layer_norm(r, lg_ref[...], lb_ref[...])
    x1_ref[...] = y
    x1b_ref[...] = y.astype(BF16)


def _resident(shape):
    return pl.BlockSpec(shape, lambda i: (0,) * len(shape), pipeline_mode=pl.Buffered(1))


def _merge(yn, attn, gates, x, ws, wa, wo, ln_g, ln_b, tm):
    t = x.shape[0]
    row = lambda i: (i, 0)
    return pl.pallas_call(
        _merge_kernel,
        out_shape=(jax.ShapeDtypeStruct((t, D_MODEL), F32), jax.ShapeDtypeStruct((t, D_MODEL), BF16)),
        grid=(t // tm,),
        in_specs=[pl.BlockSpec((tm, D_INNER), row),
                  pl.BlockSpec((tm, SB_HEADS * SB_HEAD_DIM), row),
                  pl.BlockSpec((tm, 2 * D_MODEL), row),
                  pl.BlockSpec((tm, D_MODEL), row),
                  _resident((D_INNER, D_MODEL)),
                  _resident((SB_HEADS * SB_HEAD_DIM, D_MODEL)),
                  _resident((D_MODEL, D_MODEL)),
                  _resident((1, D_MODEL)),
                  _resident((1, D_MODEL))],
        out_specs=(pl.BlockSpec((tm, D_MODEL), row), pl.BlockSpec((tm, D_MODEL), row)),
        compiler_params=_params(("parallel",)),
        name="merge_outproj_ln",
    )(yn, attn, gates, x, ws, wa, wo, ln_g.reshape(1, D_MODEL), ln_b.reshape(1, D_MODEL))


MLP_FC = 512


def _mlp_kernel(x1_ref, x1b_ref, wu_ref, wd_ref, lg_ref, lb_ref, y_ref):
    xb = x1b_ref[...]
    acc = jnp.zeros(x1_ref.shape, F32)
    for c in range(D_FF // MLP_FC):
        sl = slice(c * MLP_FC, (c + 1) * MLP_FC)
        hid = jnp.square(jnp.maximum(_dot(xb, wu_ref[:, sl]), 0.0))
        acc = acc + _dot(hid.astype(BF16), wd_ref[sl, :])
    r = DEEPNORM_ALPHA * x1_ref[...] + acc
    y_ref[...] = _layer_norm(r, lg_ref[...], lb_ref[...])


def _mlp(x1, x1b, wu, wd, ln_g, ln_b, tm):
    t = x1.shape[0]
    row = lambda i: (i, 0)
    return pl.pallas_call(
        _mlp_kernel,
        out_shape=jax.ShapeDtypeStruct((t, D_MODEL), F32),
        grid=(t // tm,),
        in_specs=[pl.BlockSpec((tm, D_MODEL), row),
                  pl.BlockSpec((tm, D_MODEL), row),
                  _resident((D_MODEL, D_FF)),
                  _resident((D_FF, D_MODEL)),
                  _resident((1, D_MODEL)),
                  _resident((1, D_MODEL))],
        out_specs=pl.BlockSpec((tm, D_MODEL), row),
        compiler_params=_params(("parallel",)),
        name="mlp_ln",
    )(x1, x1b, wu, wd, ln_g.reshape(1, D_MODEL), ln_b.reshape(1, D_MODEL))


def _row_tile(t, want):
    tm = min(t, want)
    assert t % tm == 0, (t, tm)
    return tm


def kernel(x_prompt, x_sample, cache_k, cache_v, state_ssm, state_conv, page_table, w_in, conv_w, conv_b,
           dt_bias, a_log, d_skip, ssm_norm_w, w_ssm_br, w_attn_br, sb_logit_bias, w_out, ln1_g, ln1_b,
           w_up, w_down, ln2_g, ln2_b):
    depth = w_in.shape[0]
    assert depth == 1, "single-layer step"
    bsz, seqlen, _ = x_prompt.shape
    db, dseq, _ = x_sample.shape
    assert dseq == 1 and seqlen % ATT_KB == 0 and page_table.shape[1] % DEC_PPS == 0 and db % DEC_BT == 0
    tp = bsz * seqlen

    w_r, wkv_t = _prep_w_in(w_in[0])
    ws = w_ssm_br[0].astype(BF16)
    wa = w_attn_br[0].astype(BF16)
    wo = w_out[0].astype(BF16)
    wu = w_up[0].astype(BF16)
    wd = w_down[0].astype(BF16)
    ssm_p = (conv_w[0], conv_b[0], dt_bias[0], a_log[0], d_skip[0], ssm_norm_w[0])

    def trunk_tail(yn, attn, gates, x, tm):
        x1, x1b = _merge(yn, attn, gates, x, ws, wa, wo, ln1_g[0], ln1_b[0], tm)
        return _mlp(x1, x1b, wu, wd, ln2_g[0], ln2_b[0], tm)

    xp = x_prompt.reshape(tp, D_MODEL)
    z, xbc, qs, gates, dt, kt, vt, kb, vb = _in_projection(
        xp, w_r, wkv_t, bsz, seqlen, _row_tile(seqlen, 1024), True)
    yn, ssm_prompt, conv_prompt = _ssd_prompt(xbc, dt, z, *ssm_p, bsz, seqlen)
    attn = _attn_prompt(qs, kb, vb, sb_logit_bias[0], bsz, seqlen)
    y_prompt = trunk_tail(yn, attn, gates, xp, _row_tile(tp, 512)).reshape(bsz, seqlen, D_MODEL)

    xs_ = x_sample.reshape(db, D_MODEL)
    z, xbc, qs, gates, dt, kts, vts = _in_projection(xs_, w_r, wkv_t, 1, db, db, False)
    conv_prev = jnp.transpose(state_conv[0], (1, 0, 2))
    yn, ssm_sample, conv_sample = _ssd_decode(xbc, conv_prev, dt, z, state_ssm[0], *ssm_p)
    attn = _attn_decode(qs, cache_k[0], cache_v[0], page_table, sb_logit_bias[0])
    y_sample = trunk_tail(yn, attn, gates, xs_, db).reshape(db, 1, D_MODEL)

    def token_major(xt, shape):
        n, _, s = xt.shape
        x = xt.reshape(n, SB_KV_HEADS, SB_HEAD_DIM, s)
        return jnp.transpose(x, (0, 3, 1, 2)).reshape(shape)

    kv_shape_p = (1, bsz, seqlen, SB_KV_HEADS, SB_HEAD_DIM)
    kv_shape_s = (1, db, 1, SB_KV_HEADS, SB_HEAD_DIM)
    return (y_prompt, y_sample, token_major(kt, kv_shape_p), token_major(vt, kv_shape_p),
            ssm_prompt[None], conv_prompt[None], token_major(kts, kv_shape_s),
            token_major(vts, kv_shape_s), ssm_sample[None],
            jnp.transpose(conv_sample, (1, 0, 2))[None])
```

```python
import functools

import jax
import jax.numpy as jnp
from jax import lax
from jax.experimental import pallas as pl
from jax.experimental.pallas import tpu as pltpu

F32 = jnp.float32
BF16 = jnp.bfloat16

D_MODEL = 1024
D_INNER = 2048
SSM_HEAD_DIM = 64
SSM_HEADS = 32
SSM_GROUPS = 4
SSM_HPG = SSM_HEADS // SSM_GROUPS
D_STATE = 128
CONV_W = 4
CONV_DIM = D_INNER + 2 * SSM_GROUPS * D_STATE
SSD_CHUNK = 128
SB_HEAD_DIM = 64
SB_HEADS = 16
SB_KV_HEADS = 4
SB_REP = SB_HEADS // SB_KV_HEADS
SB_SCALE = SB_HEAD_DIM ** -0.5
D_FF = 4 * D_MODEL
PAGE_SIZE = 128
DEEPNORM_ALPHA = 2.0 ** 0.25
LN_EPS = 1e-5
RMS_EPS = 1e-5
IN_SIZES = (D_INNER, CONV_DIM, SSM_HEADS, SB_HEADS * SB_HEAD_DIM,
            SB_KV_HEADS * SB_HEAD_DIM, SB_KV_HEADS * SB_HEAD_DIM, 2 * D_MODEL)

LANES = 128
VMEM_LIMIT_BYTES = 56 * 1024 * 1024

PROJ_TN = 512
_TZ = D_INNER // PROJ_TN
_TX = CONV_DIM // PROJ_TN
_TQ = (SB_HEADS * SB_HEAD_DIM) // PROJ_TN
_TG = (2 * D_MODEL) // PROJ_TN
_J_XBC = _TZ
_J_Q = _J_XBC + _TX
_J_G = _J_Q + _TQ
_J_DT = _J_G + _TG
_J_KV = _J_DT + 1
PROJ_TILES = _J_KV + 1
KV_W = SB_KV_HEADS * SB_HEAD_DIM

ATT_QB = 256
ATT_KB = 256
DEC_PPS = 32

_NT = (((1,), (1,)), ((), ()))


def _dot(a, b):
    return jnp.dot(a, b, preferred_element_type=F32)


def _dot_nt(a, b):
    return lax.dot_general(a, b, _NT, preferred_element_type=F32)


NEG_LOG2E = -1.4426950408889634


def _sigmoid(x):
    return 1.0 / (1.0 + jnp.exp2(x * NEG_LOG2E))


def _silu(x):
    return x * _sigmoid(x)


def _layer_norm(r, g, b):
    mu = jnp.mean(r, axis=-1, keepdims=True)
    d = r - mu
    var = jnp.mean(d * d, axis=-1, keepdims=True)
    return d * lax.rsqrt(var + LN_EPS) * g + b


def _split3(x):
    hi = x.astype(BF16)
    r = x - hi.astype(F32)
    mid = r.astype(BF16)
    lo = (r - mid.astype(F32)).astype(BF16)
    return hi, mid, lo


def _params(sem):
    return pltpu.CompilerParams(dimension_semantics=sem, vmem_limit_bytes=VMEM_LIMIT_BYTES)


CONV_RC = 128


def _inproj_kernel(x_ref, w_ref, wkv_ref, cw_ref, cbias_ref, *refs, key_blocks, tpb):
    if key_blocks:
        (z_ref, xbc_ref, q_ref, g_ref, dt_ref, kt_ref, vt_ref, kb_ref, vb_ref, cs_ref,
         xb_ref, win_a, win_b, ctail) = refs
    else:
        z_ref, xbc_ref, q_ref, g_ref, dt_ref, kt_ref, vt_ref, xb_ref = refs
    j = pl.program_id(1)
    tm = x_ref.shape[0]

    @pl.when(j == 0)
    def _():
        xb_ref[...] = x_ref[...].astype(BF16)

    def mm():
        return _dot(xb_ref[...], w_ref[...])

    @pl.when(j < _J_XBC)
    def _():
        z_ref[...] = mm()

    def conv_previous_tile(win):
        tail = CONV_W - 1
        t = j - 1 - _J_XBC
        seq_start = pl.program_id(0) % tpb == 0
        win[0:8, :] = jnp.where(seq_start, 0.0, ctail[t])
        for rc in range(tm // CONV_RC):
            wv = win[rc * CONV_RC:rc * CONV_RC + 8 + CONV_RC, :]
            a = cbias_ref[...]
            for tap in range(CONV_W):
                shifted = wv if tap == tail else pltpu.roll(wv, tail - tap, axis=0)
                a = a + cw_ref[tap:tap + 1, :] * shifted[8:, :]
            xbc_ref[rc * CONV_RC:(rc + 1) * CONV_RC, :] = _silu(a)
        ctail[t] = win[tm:tm + 8, :]
        cs_ref[0] = win[8 + tm - tail:8 + tm, :]

    if key_blocks:
        tx = j - _J_XBC

        @pl.when(tx == 0)
        def _():
            win_a[8:8 + tm, :] = mm()

        @pl.when((tx > 0) & (tx < _TX) & (tx % 2 == 1))
        def _():
            conv_previous_tile(win_a)
            win_b[8:8 + tm, :] = mm()

        @pl.when((tx > 0) & (tx < _TX) & (tx % 2 == 0))
        def _():
            conv_previous_tile(win_b)
            win_a[8:8 + tm, :] = mm()

        @pl.when(j == _J_Q)
        def _():
            conv_previous_tile(win_b if _TX % 2 == 0 else win_a)
            q_ref[...] = (mm() * SB_SCALE).astype(BF16)

        @pl.when((j > _J_Q) & (j < _J_G))
        def _():
            q_ref[...] = (mm() * SB_SCALE).astype(BF16)
    else:
        @pl.when((j >= _J_XBC) & (j < _J_Q))
        def _():
            xbc_ref[...] = mm()

        @pl.when((j >= _J_Q) & (j < _J_G))
        def _():
            q_ref[...] = (mm() * SB_SCALE).astype(BF16)

    @pl.when((j >= _J_G) & (j < _J_DT))
    def _():
        g_ref[...] = mm()

    @pl.when(j == _J_DT)
    def _():
        dt_ref[...] = mm()[:, :LANES]

    @pl.when(j == _J_KV)
    def _():
        kvt = _dot_nt(wkv_ref[...], xb_ref[...])
        kt_ref[0] = kvt[:KV_W]
        vt_ref[0] = kvt[KV_W:]
        if key_blocks:
            for c in range(key_blocks):
                csl = slice(c * ATT_KB, (c + 1) * ATT_KB)
                for g in range(SB_KV_HEADS):
                    kb_ref[0, c, g] = kvt[g * SB_HEAD_DIM:(g + 1) * SB_HEAD_DIM, csl].astype(BF16)
                    vb_ref[0, c, g] = kvt[KV_W + g * SB_HEAD_DIM:KV_W + (g + 1) * SB_HEAD_DIM, csl].astype(BF16)


def _prep_w_in(w_in):
    splits = []
    off = 0
    for s in IN_SIZES:
        splits.append(w_in[:, off:off + s])
        off += s
    wz, wxbc, wdt, wq, wk, wv, wg = splits
    wdt = jnp.pad(wdt, ((0, 0), (0, PROJ_TN - SSM_HEADS)))
    w_r = jnp.concatenate([wz, wxbc, wq, wg, wdt], axis=1).astype(BF16)
    wkv_t = jnp.concatenate([wk, wv], axis=1).T.astype(BF16)
    return w_r, wkv_t


def _in_projection(x, w_r, wkv_t, conv_w, conv_b, bsz, seqlen, tm, prompt):
    t = x.shape[0]
    tpb = seqlen // tm
    key_blocks = tm // ATT_KB if prompt else 0

    def tile(lo, n):
        return lambda i, j: (i, jnp.clip(j - lo, 0, n - 1))

    lag = 1 if prompt else 0
    xbc_tile = lambda i, j: (0, jnp.clip(j - _J_XBC - lag, 0, _TX - 1))
    row = lambda i, j: (i, 0)
    seq3 = lambda i, j: (i // tpb, 0, i % tpb)
    seq5 = lambda i, j: (i // tpb, i % tpb, 0, 0, 0)
    out_shape = [
        jax.ShapeDtypeStruct((t, D_INNER), F32),
        jax.ShapeDtypeStruct((t, CONV_DIM), F32),
        jax.ShapeDtypeStruct((t, SB_HEADS * SB_HEAD_DIM), BF16),
        jax.ShapeDtypeStruct((t, 2 * D_MODEL), F32),
        jax.ShapeDtypeStruct((t, LANES), F32),
        jax.ShapeDtypeStruct((bsz, KV_W, seqlen), F32),
        jax.ShapeDtypeStruct((bsz, KV_W, seqlen), F32),
    ]
    out_specs = [
        pl.BlockSpec((tm, PROJ_TN), tile(0, _TZ)),
        pl.BlockSpec((tm, PROJ_TN), tile(_J_XBC + lag, _TX)),
        pl.BlockSpec((tm, PROJ_TN), tile(_J_Q, _TQ)),
        pl.BlockSpec((tm, PROJ_TN), tile(_J_G, _TG)),
        pl.BlockSpec((tm, LANES), row),
        pl.BlockSpec((1, KV_W, tm), seq3),
        pl.BlockSpec((1, KV_W, tm), seq3),
    ]
    scratch = [pltpu.VMEM((tm, D_MODEL), BF16)]
    if prompt:
        blk = (bsz, seqlen // ATT_KB, SB_KV_HEADS, SB_HEAD_DIM, ATT_KB)
        out_shape += [jax.ShapeDtypeStruct(blk, BF16)] * 2
        out_specs += [pl.BlockSpec((1, key_blocks, SB_KV_HEADS, SB_HEAD_DIM, ATT_KB), seq5)] * 2
        out_shape.append(jax.ShapeDtypeStruct((t // tm, CONV_W - 1, CONV_DIM), F32))
        out_specs.append(pl.BlockSpec((1, CONV_W - 1, PROJ_TN),
                                      lambda i, j: (i, 0, jnp.clip(j - _J_XBC - lag, 0, _TX - 1))))
        scratch += [pltpu.VMEM((tm + 8, PROJ_TN), F32),
                    pltpu.VMEM((tm + 8, PROJ_TN), F32),
                    pltpu.VMEM((_TX, 8, PROJ_TN), F32)]
    return pl.pallas_call(
        functools.partial(_inproj_kernel, key_blocks=key_blocks, tpb=tpb),
        out_shape=tuple(out_shape),
        grid=(t // tm, PROJ_TILES),
        in_specs=[pl.BlockSpec((tm, D_MODEL), row),
                  pl.BlockSpec((D_MODEL, PROJ_TN), lambda i, j: (0, jnp.minimum(j, _J_KV - 1))),
                  pl.BlockSpec((2 * KV_W, D_MODEL), lambda i, j: (0, 0)),
                  pl.BlockSpec((CONV_W, PROJ_TN), xbc_tile),
                  pl.BlockSpec((1, PROJ_TN), xbc_tile)],
        out_specs=tuple(out_specs),
        scratch_shapes=scratch,
        compiler_params=_params(("arbitrary", "arbitrary")),
        name="in_projection",
    )(x, w_r, wkv_t, conv_w, conv_b.reshape(1, CONV_DIM))


def _softplus(x):
    return jnp.maximum(x, 0.0) + jnp.log1p(jnp.exp(-jnp.abs(x)))


def _gated_rms_norm(y, z, w):
    h = y * _silu(z)
    h = h * lax.rsqrt(jnp.mean(h * h, axis=-1, keepdims=True) + RMS_EPS)
    return h * w


def _head_lanes(x, sel):
    pieces = [p[:, :SSM_HEADS] for p in _split3(x)]
    pad = jnp.zeros((x.shape[0], LANES - 3 * SSM_HEADS), BF16)
    return _dot(jnp.concatenate(pieces + [pad], axis=1), sel)


def _head_selector(lanes_per_head):
    k = jnp.arange(LANES)[:, None]
    n = jnp.arange(SSM_HEADS * lanes_per_head)[None, :]
    return ((k % SSM_HEADS == n // lanes_per_head) & (k < 3 * SSM_HEADS)).astype(BF16)


def _ssd_prompt_kernel(xc, dt_ref, z_ref, dtb_ref, alog_ref, dskip_ref, nw_ref,
                       selp_ref, sels_ref, yn_ref, hout_ref,
                       hst, ybuf, dtx_s, ecx_s, tex_s, colb_s):
    c = pl.program_id(1)
    nc = pl.num_programs(1)
    q = SSD_CHUNK

    @pl.when(c == 0)
    def _():
        hst[...] = jnp.zeros(hst.shape, F32)

    dtv = _softplus(dt_ref[...] + dtb_ref[...])
    neg_a = -jnp.exp(alog_ref[...])
    da = dtv * neg_a
    rows = lax.broadcasted_iota(jnp.int32, (q, q), 0)
    cols = lax.broadcasted_iota(jnp.int32, (q, q), 1)
    causal = rows >= cols
    tri = causal.astype(BF16)
    hi, mid, lo = _split3(da)
    cum = (_dot(tri, lo) + _dot(tri, mid)) + _dot(tri, hi)
    cum_t = cum.T
    cumx = _head_lanes(cum, selp_ref[...])
    cum_last_x = cumx[q - 1:q, :]
    dtx_s[...] = _head_lanes(dtv, selp_ref[...])
    ecx_s[...] = jnp.exp(cumx)
    tex_s[...] = jnp.exp(cum_last_x - cumx)
    cdec_x = jnp.exp(cum_last_x)
    colb_s[...] = _head_lanes(cum, sels_ref[...])

    first_head = lax.broadcasted_iota(jnp.int32, (q, 2 * SSM_HEAD_DIM), 1) < SSM_HEAD_DIM
    bm_off = D_INNER
    cm_off = D_INNER + SSM_GROUPS * D_STATE
    pairs_per_group = SSM_HPG // 2
    for g in range(SSM_GROUPS):
        bg = xc[:, bm_off + g * D_STATE:bm_off + (g + 1) * D_STATE]
        cg = xc[:, cm_off + g * D_STATE:cm_off + (g + 1) * D_STATE]
        bgb = bg.astype(BF16)
        cgb = cg.astype(BF16)
        cb = _dot_nt(cgb, bgb)
        bgtb = bg.T.astype(BF16)
        for r in range(pairs_per_group):
            pair = g * pairs_per_group + r
            psl = slice(pair * 2 * SSM_HEAD_DIM, (pair + 1) * 2 * SSM_HEAD_DIM)
            ms = []
            for h in (2 * pair, 2 * pair + 1):
                seg = colb_s[:, h * q:(h + 1) * q] - cum_t[h:h + 1, :]
                decay = jnp.exp(jnp.where(causal, seg, -jnp.inf))
                ms.append((decay * cb).astype(BF16))
            xp = xc[:, psl]
            xdt = xp * dtx_s[:, psl]
            xdtb = xdt.astype(BF16)
            zero = jnp.zeros_like(xdtb)
            rhs = jnp.concatenate([jnp.where(first_head, xdtb, zero), jnp.where(first_head, zero, xdtb)],
                                  axis=0)
            y_diag = _dot(jnp.concatenate(ms, axis=1), rhs)
            h_in = hst[pair]
            y_off = _dot(cgb, h_in.astype(BF16)) * ecx_s[:, psl]
            st = _dot(bgtb, (xdt * tex_s[:, psl]).astype(BF16))
            hst[pair] = h_in * cdec_x[:, psl] + st
            ybuf[:, psl] = (y_diag + y_off) + xp * dskip_ref[:, psl]

    yn_ref[...] = _gated_rms_norm(ybuf[...], z_ref[...], nw_ref[...]).astype(BF16)

    @pl.when(c == nc - 1)
    def _():
        for pair in range(SSM_HEADS // 2):
            ht = hst[pair].T
            hout_ref[0, 2 * pair] = ht[:SSM_HEAD_DIM]
            hout_ref[0, 2 * pair + 1] = ht[SSM_HEAD_DIM:]


def _pad_lanes(v):
    return jnp.pad(v, (0, LANES - v.shape[0])).reshape(1, LANES)


def _ssd_prompt(xc, dt, z, dt_bias, a_log, d_skip, norm_w, bsz, seqlen):
    t = bsz * seqlen
    q = SSD_CHUNK
    nc = seqlen // q
    tok = lambda b, c: (b * nc + c, 0)
    const = lambda b, c: (0, 0)
    out_shape = (
        jax.ShapeDtypeStruct((t, D_INNER), BF16),
        jax.ShapeDtypeStruct((bsz, SSM_HEADS, SSM_HEAD_DIM, D_STATE), F32),
    )
    return pl.pallas_call(
        _ssd_prompt_kernel,
        out_shape=out_shape,
        grid=(bsz, nc),
        in_specs=[pl.BlockSpec((q, CONV_DIM), tok),
                  pl.BlockSpec((q, LANES), tok),
                  pl.BlockSpec((q, D_INNER), tok),
                  pl.BlockSpec((1, LANES), const),
                  pl.BlockSpec((1, LANES), const),
                  pl.BlockSpec((1, D_INNER), const),
                  pl.BlockSpec((1, D_INNER), const),
                  pl.BlockSpec((LANES, D_INNER), const),
                  pl.BlockSpec((LANES, SSM_HEADS * q), const)],
        out_specs=(pl.BlockSpec((q, D_INNER), tok),
                   pl.BlockSpec((1, SSM_HEADS, SSM_HEAD_DIM, D_STATE), lambda b, c: (b, 0, 0, 0))),
        scratch_shapes=[pltpu.VMEM((SSM_HEADS // 2, D_STATE, 2 * SSM_HEAD_DIM), F32),
                        pltpu.VMEM((q, D_INNER), F32),
                        pltpu.VMEM((q, D_INNER), F32),
                        pltpu.VMEM((q, D_INNER), F32),
                        pltpu.VMEM((q, D_INNER), F32),
                        pltpu.VMEM((q, SSM_HEADS * q), F32)],
        compiler_params=_params(("parallel", "arbitrary")),
        name="ssd_prompt",
    )(xc, dt, z, _pad_lanes(dt_bias), _pad_lanes(a_log),
      jnp.repeat(d_skip, SSM_HEAD_DIM).reshape(1, D_INNER), norm_w.reshape(1, D_INNER),
      _head_selector(SSM_HEAD_DIM), _head_selector(q))


SOFTPLUS_LINEAR = 30.0


def _sb_weights(zs, keeps, tmat, carry, stacked):
    rows = zs[0].shape[0]
    sps, log_betas, firsts = [], [], []
    for z, keep in zip(zs, keeps):
        sp = jnp.maximum(z, jnp.log(1.0 + jnp.exp(jnp.minimum(z, SOFTPLUS_LINEAR))))
        log_betas.append(z - sp)
        if keep is not None:
            sp = sp * keep
        sps.append(sp.astype(BF16))
        firsts.append(sp[:, 0:1])
    if stacked:
        cum = _dot(jnp.concatenate(sps, axis=0), tmat)
        cums = [cum[n * rows:(n + 1) * rows] for n in range(len(zs))]
    else:
        cums = [_dot(sp, tmat) for sp in sps]
    ws = []
    for log_beta, c, first, keep in zip(log_betas, cums, firsts, keeps):
        w = jnp.exp(log_beta - c - carry)
        ws.append(w if keep is None else w * keep)
        carry = carry + (c[:, 0:1] + first)
    return ws, carry


def _later_key_matrix(n):
    r = jnp.arange(n)[:, None]
    c = jnp.arange(n)[None, :]
    return (r > c).astype(BF16)


def _attn_prompt_kernel(q_ref, k_ref, v_ref, baug_ref, keep_ref, tmat_ref, ones_ref, o_ref,
                        acc_ref, carry_ref):
    i = pl.program_id(2)
    qb = q_ref[...]
    qs = jnp.concatenate([qb[:, r * SB_HEAD_DIM:(r + 1) * SB_HEAD_DIM] for r in range(SB_REP)], axis=0)
    q_aug = jnp.concatenate([qs, baug_ref[0]], axis=1)
    tmat = tmat_ref[...]

    def blocks(jbs, keeps):
        ones = ones_ref[...]
        zs = [_dot(q_aug, jnp.concatenate([k_ref[0, jb, 0], ones], axis=0)) for jb in jbs]
        ws, carry = _sb_weights(zs, keeps, tmat, carry_ref[...], False)
        pv = None
        for jb, w in zip(jbs, ws):
            pv_b = _dot_nt(w.astype(BF16), v_ref[0, jb, 0])
            pv = pv_b if pv is None else pv + pv_b
        acc_ref[...] += pv
        carry_ref[...] = carry

    acc_ref[...] = jnp.zeros(acc_ref.shape, F32)
    carry_ref[...] = jnp.zeros(carry_ref.shape, F32)
    per = ATT_KB // ATT_QB
    jd = i // per
    blocks([jd], [keep_ref[i - jd * per]])

    @pl.when(jd % 2 == 1)
    def _():
        blocks([jd - 1], [None])

    npair = jd // 2

    def body(t, _):
        hi = 2 * (npair - t) - 1
        blocks([hi, hi - 1], [None, None])
        return 0

    lax.fori_loop(0, npair, body, 0)

    for r in range(SB_REP):
        o_ref[:, r * SB_HEAD_DIM:(r + 1) * SB_HEAD_DIM] = acc_ref[r * ATT_QB:(r + 1) * ATT_QB, :].astype(BF16)


def _attn_prompt(qs, kb, vb, bias, bsz, seqlen):
    t = bsz * seqlen
    nq = seqlen // ATT_QB
    nkb = seqlen // ATT_KB
    rows = SB_REP * ATT_QB
    gw = SB_REP * SB_HEAD_DIM
    parts = jnp.stack(_split3(bias.astype(F32)), axis=-1)
    parts = jnp.pad(parts, ((0, 0), (0, SB_HEAD_DIM - 3)))
    bias_aug = jnp.broadcast_to(parts.reshape(SB_KV_HEADS, SB_REP, 1, SB_HEAD_DIM),
                                (SB_KV_HEADS, SB_REP, ATT_QB, SB_HEAD_DIM)).reshape(
                                    SB_KV_HEADS, rows, SB_HEAD_DIM)
    ones_rows = jnp.broadcast_to((jnp.arange(SB_HEAD_DIM) < 3).astype(BF16)[:, None],
                                 (SB_HEAD_DIM, ATT_KB))
    qpos = jnp.arange(rows) % ATT_QB
    keep = jnp.stack([(jnp.arange(ATT_KB)[None, :] < (par * ATT_QB + qpos)[:, None]).astype(F32)
                      for par in range(ATT_KB // ATT_QB)])
    const2 = lambda b, g, i: (0, 0)
    return pl.pallas_call(
        _attn_prompt_kernel,
        out_shape=jax.ShapeDtypeStruct((t, SB_HEADS * SB_HEAD_DIM), BF16),
        grid=(bsz, SB_KV_HEADS, nq),
        in_specs=[pl.BlockSpec((ATT_QB, gw), lambda b, g, i: (b * nq + i, g)),
                  pl.BlockSpec((1, nkb, 1, SB_HEAD_DIM, ATT_KB), lambda b, g, i: (b, 0, g, 0, 0)),
                  pl.BlockSpec((1, nkb, 1, SB_HEAD_DIM, ATT_KB), lambda b, g, i: (b, 0, g, 0, 0)),
                  pl.BlockSpec((1, rows, SB_HEAD_DIM), lambda b, g, i: (g, 0, 0)),
                  pl.BlockSpec((ATT_KB // ATT_QB, rows, ATT_KB), lambda b, g, i: (0, 0, 0)),
                  pl.BlockSpec((ATT_KB, ATT_KB), const2),
                  pl.BlockSpec((SB_HEAD_DIM, ATT_KB), const2)],
        out_specs=pl.BlockSpec((ATT_QB, gw), lambda b, g, i: (b * nq + i, g)),
        scratch_shapes=[pltpu.VMEM((rows, SB_HEAD_DIM), F32),
                        pltpu.VMEM((rows, 1), F32)],
        compiler_params=_params(("parallel", "parallel", "arbitrary")),
        name="attn_prompt",
    )(qs, kb, vb, bias_aug, keep, _later_key_matrix(ATT_KB), ones_rows)


def _attn_decode_kernel(pt_ref, q_ref, bias_ref, tmat_ref, *refs):
    k_refs = refs[:DEC_PPS]
    v_refs = refs[DEC_PPS:2 * DEC_PPS]
    o_ref, acc_ref, carry_ref = refs[2 * DEC_PPS:]
    s = pl.program_id(1)

    @pl.when(s == 0)
    def _():
        acc_ref[...] = jnp.zeros(acc_ref.shape, F32)
        carry_ref[...] = jnp.zeros(carry_ref.shape, F32)

    hq = lax.broadcasted_iota(jnp.int32, (SB_HEADS, KV_W), 0) // SB_REP
    gl = lax.broadcasted_iota(jnp.int32, (SB_HEADS, KV_W), 1) // SB_HEAD_DIM
    own = hq == gl
    q_t = jnp.concatenate([q_ref[0]] * SB_KV_HEADS, axis=1)
    q_bd = jnp.where(own, q_t, jnp.zeros_like(q_t))
    bias = bias_ref[...]
    zs = [_dot(q_bd, k_refs[m][0].astype(BF16)) + bias for m in range(DEC_PPS)]
    ws, carry = _sb_weights(zs, [None] * DEC_PPS, tmat_ref[...], carry_ref[...], True)
    w = jnp.concatenate([w.astype(BF16) for w in ws], axis=1)
    vt = jnp.concatenate([v_refs[m][0].astype(BF16) for m in range(DEC_PPS)], axis=1)
    acc_ref[...] += _dot_nt(w, vt)
    carry_ref[...] = carry

    @pl.when(s == pl.num_programs(1) - 1)
    def _():
        acc = jnp.where(own, acc_ref[...], 0.0)
        out = acc[:, :SB_HEAD_DIM]
        for g in range(1, SB_KV_HEADS):
            out = out + acc[:, g * SB_HEAD_DIM:(g + 1) * SB_HEAD_DIM]
        o_ref[0] = out


def _attn_decode(qs, cache_k, cache_v, page_table, bias):
    db, n_pages = page_table.shape
    n_pool = cache_k.shape[0]
    ck = jnp.transpose(cache_k, (0, 2, 3, 1)).reshape(n_pool, KV_W, PAGE_SIZE)
    cv = jnp.transpose(cache_v, (0, 2, 3, 1)).reshape(n_pool, KV_W, PAGE_SIZE)
    pt_t = page_table.T
    q3 = qs.reshape(db, SB_HEADS, SB_HEAD_DIM)
    bias_rows = jnp.broadcast_to(bias.astype(F32).reshape(SB_HEADS, 1), (SB_HEADS, PAGE_SIZE))

    def page(m):
        return lambda b, s, pt: (pt[n_pages - 1 - (s * DEC_PPS + m), b], 0, 0)

    page_specs = [pl.BlockSpec((1, KV_W, PAGE_SIZE), page(m)) for m in range(DEC_PPS)]
    grid_spec = pltpu.PrefetchScalarGridSpec(
        num_scalar_prefetch=1,
        grid=(db, n_pages // DEC_PPS),
        in_specs=[pl.BlockSpec((1, SB_HEADS, SB_HEAD_DIM), lambda b, s, pt: (b, 0, 0)),
                  pl.BlockSpec((SB_HEADS, PAGE_SIZE), lambda b, s, pt: (0, 0)),
                  pl.BlockSpec((PAGE_SIZE, PAGE_SIZE), lambda b, s, pt: (0, 0))] + page_specs + page_specs,
        out_specs=pl.BlockSpec((1, SB_HEADS, SB_HEAD_DIM), lambda b, s, pt: (b, 0, 0)),
        scratch_shapes=[pltpu.VMEM((SB_HEADS, KV_W), F32),
                        pltpu.VMEM((SB_HEADS, 1), F32)],
    )
    out = pl.pallas_call(
        _attn_decode_kernel,
        out_shape=jax.ShapeDtypeStruct((db, SB_HEADS, SB_HEAD_DIM), F32),
        grid_spec=grid_spec,
        compiler_params=_params(("parallel", "arbitrary")),
        name="attn_decode",
    )(pt_t, q3, bias_rows, _later_key_matrix(PAGE_SIZE), *([ck] * DEC_PPS), *([cv] * DEC_PPS))
    return out.reshape(db, SB_HEADS * SB_HEAD_DIM)


DEC_BT = 8


def _ssd_decode_kernel(xbc_ref, cprev_ref, dt_ref, z_ref, h0_ref, cw_ref, cb_ref, dtb_ref, alog_ref,
                       dskip_ref, nw_ref, yn_ref, hout_ref, cout_ref, ybuf):
    bt = DEC_BT
    x_new = xbc_ref[...]
    a = cb_ref[...]
    for tap in range(CONV_W - 1):
        a = a + cw_ref[tap:tap + 1, :] * cprev_ref[tap]
        if tap > 0:
            cout_ref[tap - 1] = cprev_ref[tap]
    a = a + cw_ref[CONV_W - 1:CONV_W, :] * x_new
    cout_ref[CONV_W - 2] = x_new
    xc = _silu(a)
    xs = xc[:, :D_INNER]
    bm = xc[:, D_INNER:D_INNER + SSM_GROUPS * D_STATE]
    cm = xc[:, D_INNER + SSM_GROUPS * D_STATE:]
    dtv = _softplus(dt_ref[...] + dtb_ref[...])
    da = jnp.exp(dtv * (-jnp.exp(alog_ref[...])))
    bmb = bm.astype(BF16)
    cmb = cm.astype(BF16)
    sub = lax.broadcasted_iota(jnp.int32, (bt, SSM_HEAD_DIM), 0)
    for g in range(SSM_GROUPS):
        gsl = slice(g * D_STATE, (g + 1) * D_STATE)
        bgb = bmb[:, gsl]
        cgb = cmb[:, gsl]
        cbg = jnp.sum(cgb.astype(F32) * bgb.astype(F32), axis=-1, keepdims=True)
        cbg = cbg.astype(BF16).astype(F32)
        for r in range(SSM_HPG):
            h = g * SSM_HPG + r
            hsl = slice(h * SSM_HEAD_DIM, (h + 1) * SSM_HEAD_DIM)
            xh = xs[:, hsl]
            xdt = xh * dtv[:, h:h + 1]
            xdtb = xdt.astype(BF16)
            y_diag = cbg * xdtb.astype(F32)
            for b in range(bt):
                h_in = h0_ref[b, h]
                y_off = _dot_nt(cgb, h_in.astype(BF16))
                x_one = jnp.where(sub == b, xdtb, jnp.zeros_like(xdtb))
                st = lax.dot_general(x_one, bgb, (((0,), (0,)), ((), ())),
                                     preferred_element_type=F32)
                hout_ref[b, h] = h_in * da[b:b + 1, h:h + 1] + st
                ybuf[b:b + 1, hsl] = y_off[b:b + 1, :] * da[b:b + 1, h:h + 1]
            ybuf[:, hsl] = (y_diag + ybuf[:, hsl]) + xh * dskip_ref[:, hsl]
    yn_ref[...] = _gated_rms_norm(ybuf[...], z_ref[...], nw_ref[...]).astype(BF16)


def _ssd_decode(xbc, conv_prev, dt, z, h0, conv_w, conv_b, dt_bias, a_log, d_skip, norm_w):
    db = xbc.shape[0]
    bt = DEC_BT
    row = lambda i: (i, 0)
    const = lambda i: (0, 0)
    out_shape = (
        jax.ShapeDtypeStruct((db, D_INNER), BF16),
        jax.ShapeDtypeStruct((db, SSM_HEADS, SSM_HEAD_DIM, D_STATE), F32),
        jax.ShapeDtypeStruct((CONV_W - 1, db, CONV_DIM), F32),
    )
    return pl.pallas_call(
        _ssd_decode_kernel,
        out_shape=out_shape,
        grid=(db // bt,),
        in_specs=[pl.BlockSpec((bt, CONV_DIM), row),
                  pl.BlockSpec((CONV_W - 1, bt, CONV_DIM), lambda i: (0, i, 0)),
                  pl.BlockSpec((bt, LANES), row),
                  pl.BlockSpec((bt, D_INNER), row),
                  pl.BlockSpec((bt, SSM_HEADS, SSM_HEAD_DIM, D_STATE), lambda i: (i, 0, 0, 0)),
                  pl.BlockSpec((CONV_W, CONV_DIM), const),
                  pl.BlockSpec((1, CONV_DIM), const),
                  pl.BlockSpec((1, LANES), const),
                  pl.BlockSpec((1, LANES), const),
                  pl.BlockSpec((1, D_INNER), const),
                  pl.BlockSpec((1, D_INNER), const)],
        out_specs=(pl.BlockSpec((bt, D_INNER), row),
                   pl.BlockSpec((bt, SSM_HEADS, SSM_HEAD_DIM, D_STATE), lambda i: (i, 0, 0, 0)),
                   pl.BlockSpec((CONV_W - 1, bt, CONV_DIM), lambda i: (0, i, 0))),
        scratch_shapes=[pltpu.VMEM((bt, D_INNER), F32)],
        compiler_params=_params(("parallel",)),
        name="ssd_decode",
    )(xbc, conv_prev, dt, z, h0, conv_w, conv_b.reshape(1, CONV_DIM), _pad_lanes(dt_bias),
      _pad_lanes(a_log), jnp.repeat(d_skip, SSM_HEAD_DIM).reshape(1, D_INNER),
      norm_w.reshape(1, D_INNER))


def _merge_kernel(yn_ref, at_ref, g_ref, x_ref, ws_ref, wa_ref, wo_ref, lg_ref, lb_ref,
                  x1_ref, x1b_ref):
    ssm_out = _dot(yn_ref[...], ws_ref[...])
    attn_out = _dot(at_ref[...].astype(BF16), wa_ref[...])
    mixed = _sigmoid(g_ref[:, :D_MODEL]) * ssm_out + _sigmoid(g_ref[:, D_MODEL:]) * attn_out
    r = DEEPNORM_ALPHA * x_ref[...] + _dot(mixed.astype(BF16), wo_ref[...])
    y = _layer_norm(r, lg_ref[...], lb_ref[...])
    x1_ref[...] = y
    x1b_ref[...] = y.astype(BF16)


def _resident(shape):
    return pl.BlockSpec(shape, lambda i: (0,) * len(shape), pipeline_mode=pl.Buffered(1))


def _merge(yn, attn, gates, x, ws, wa, wo, ln_g, ln_b, tm):
    t = x.shape[0]
    row = lambda i: (i, 0)
    return pl.pallas_call(
        _merge_kernel,
        out_shape=(jax.ShapeDtypeStruct((t, D_MODEL), F32), jax.ShapeDtypeStruct((t, D_MODEL), BF16)),
        grid=(t // tm,),
        in_specs=[pl.BlockSpec((tm, D_INNER), row),
                  pl.BlockSpec((tm, SB_HEADS * SB_HEAD_DIM), row),
                  pl.BlockSpec((tm, 2 * D_MODEL), row),
                  pl.BlockSpec((tm, D_MODEL), row),
                  _resident((D_INNER, D_MODEL)),
                  _resident((SB_HEADS * SB_HEAD_DIM, D_MODEL)),
                  _resident((D_MODEL, D_MODEL)),
                  _resident((1, D_MODEL)),
                  _resident((1, D_MODEL))],
        out_specs=(pl.BlockSpec((tm, D_MODEL), row), pl.BlockSpec((tm, D_MODEL), row)),
        compiler_params=_params(("parallel",)),
        name="merge_outproj_ln",
    )(yn, attn, gates, x, ws, wa, wo, ln_g.reshape(1, D_MODEL), ln_b.reshape(1, D_MODEL))


MLP_FC = 512


def _mlp_kernel(x1_ref, x1b_ref, wu_ref, wd_ref, lg_ref, lb_ref, y_ref):
    xb = x1b_ref[...]
    acc = jnp.zeros(x1_ref.shape, F32)
    for c in range(D_FF // MLP_FC):
        sl = slice(c * MLP_FC, (c + 1) * MLP_FC)
        hid = jnp.square(jnp.maximum(_dot(xb, wu_ref[:, sl]), 0.0))
        acc = acc + _dot(hid.astype(BF16), wd_ref[sl, :])
    r = DEEPNORM_ALPHA * x1_ref[...] + acc
    y_ref[...] = _layer_norm(r, lg_ref[...], lb_ref[...])


def _mlp(x1, x1b, wu, wd, ln_g, ln_b, tm):
    t = x1.shape[0]
    row = lambda i: (i, 0)
    return pl.pallas_call(
        _mlp_kernel,
        out_shape=jax.ShapeDtypeStruct((t, D_MODEL), F32),
        grid=(t // tm,),
        in_specs=[pl.BlockSpec((tm, D_MODEL), row),
                  pl.BlockSpec((tm, D_MODEL), row),
                  _resident((D_MODEL, D_FF)),
                  _resident((D_FF, D_MODEL)),
                  _resident((1, D_MODEL)),
                  _resident((1, D_MODEL))],
        out_specs=pl.BlockSpec((tm, D_MODEL), row),
        compiler_params=_params(("parallel",)),
        name="mlp_ln",
    )(x1, x1b, wu, wd, ln_g.reshape(1, D_MODEL), ln_b.reshape(1, D_MODEL))


def _row_tile(t, want):
    tm = min(t, want)
    assert t % tm == 0, (t, tm)
    return tm


def kernel(x_prompt, x_sample, cache_k, cache_v, state_ssm, state_conv, page_table, w_in, conv_w, conv_b,
           dt_bias, a_log, d_skip, ssm_norm_w, w_ssm_br, w_attn_br, sb_logit_bias, w_out, ln1_g, ln1_b,
           w_up, w_down, ln2_g, ln2_b):
    depth = w_in.shape[0]
    assert depth == 1, "single-layer step"
    bsz, seqlen, _ = x_prompt.shape
    db, dseq, _ = x_sample.shape
    assert dseq == 1 and seqlen % ATT_KB == 0 and page_table.shape[1] % DEC_PPS == 0 and db % DEC_BT == 0
    tp = bsz * seqlen

    w_r, wkv_t = _prep_w_in(w_in[0])
    ws = w_ssm_br[0].astype(BF16)
    wa = w_attn_br[0].astype(BF16)
    wo = w_out[0].astype(BF16)
    wu = w_up[0].astype(BF16)
    wd = w_down[0].astype(BF16)
    ssm_p = (conv_w[0], conv_b[0], dt_bias[0], a_log[0], d_skip[0], ssm_norm_w[0])

    def trunk_tail(yn, attn, gates, x, tm):
        x1, x1b = _merge(yn, attn, gates, x, ws, wa, wo, ln1_g[0], ln1_b[0], tm)
        return _mlp(x1, x1b, wu, wd, ln2_g[0], ln2_b[0], tm)

    xp = x_prompt.reshape(tp, D_MODEL)
    tm_p = _row_tile(seqlen, 1024)
    z, xc, qs, gates, dt, kt, vt, kb, vb, tile_tails = _in_projection(
        xp, w_r, wkv_t, conv_w[0], conv_b[0], bsz, seqlen, tm_p, True)
    conv_prompt = tile_tails[seqlen // tm_p - 1::seqlen // tm_p]
    yn, ssm_prompt = _ssd_prompt(xc, dt, z, *ssm_p[2:], bsz, seqlen)
    attn = _attn_prompt(qs, kb, vb, sb_logit_bias[0], bsz, seqlen)
    y_prompt = trunk_tail(yn, attn, gates, xp, _row_tile(tp, 512)).reshape(bsz, seqlen, D_MODEL)

    xs_ = x_sample.reshape(db, D_MODEL)
    z, xbc, qs, gates, dt, kts, vts = _in_projection(
        xs_, w_r, wkv_t, conv_w[0], conv_b[0], 1, db, db, False)
    conv_prev = jnp.transpose(state_conv[0], (1, 0, 2))
    yn, ssm_sample, conv_sample = _ssd_decode(xbc, conv_prev, dt, z, state_ssm[0], *ssm_p)
    attn = _attn_decode(qs, cache_k[0], cache_v[0], page_table, sb_logit_bias[0])
    y_sample = trunk_tail(yn, attn, gates, xs_, db).reshape(db, 1, D_MODEL)

    def token_major(xt, shape):
        n, _, s = xt.shape
        x = xt.reshape(n, SB_KV_HEADS, SB_HEAD_DIM, s)
        return jnp.transpose(x, (0, 3, 1, 2)).reshape(shape)

    kv_shape_p = (1, bsz, seqlen, SB_KV_HEADS, SB_HEAD_DIM)
    kv_shape_s = (1, db, 1, SB_KV_HEADS, SB_HEAD_DIM)
    return (y_prompt, y_sample, token_major(kt, kv_shape_p), token_major(vt, kv_shape_p),
            ssm_prompt[None], conv_prompt[None], token_major(kts, kv_shape_s),
            token_major(vts, kv_shape_s), ssm_sample[None],
            jnp.transpose(conv_sample, (1, 0, 2))[None])
```
